```python
import math
import jax, jax.numpy as jnp
from jax import lax
import numpy as np

D_MODEL = 4096
BATCH = 4
SEQ = 2048
DEPTH = 1

MIX_WIDTH = D_MODEL
GDN_HEAD_DIM = 128
GDN_WIDTH = MIX_WIDTH // 2
GDN_HEADS = GDN_WIDTH // GDN_HEAD_DIM
DIFF_HEAD_DIM = 128
DIFF_V_DIM = 2 * DIFF_HEAD_DIM
DIFF_WIDTH = MIX_WIDTH - GDN_WIDTH
DIFF_HEADS = DIFF_WIDTH // DIFF_V_DIM
DIFF_QK_WIDTH = DIFF_HEADS * 2 * DIFF_HEAD_DIM
CONV_K = 4
CHUNK = 64
Q_BLOCK = 128
D_FF = -(-8 * D_MODEL // (3 * 256)) * 256
DEEPNORM_ALPHA = (2 * DEPTH) ** 0.25
DEEPNORM_BETA = (8 * DEPTH) ** -0.25

IN_SPLITS = (GDN_WIDTH, GDN_WIDTH, GDN_WIDTH, GDN_WIDTH, GDN_HEADS, GDN_HEADS,
             DIFF_QK_WIDTH, DIFF_QK_WIDTH, DIFF_WIDTH)
IN_WIDTH = sum(IN_SPLITS)
VALUE_SLOTS = (2, 8)

kernel_name = "hymba_gdn_diffattn_deepnorm_block"


def layer_norm(x, g, b, eps=1e-5):
    xf = x.astype(jnp.float32)
    mu = jnp.mean(xf, -1, keepdims=True)
    var = jnp.mean(jnp.square(xf - mu), -1, keepdims=True)
    return ((xf - mu) * lax.rsqrt(var + eps) * g.astype(jnp.float32) + b.astype(jnp.float32)).astype(x.dtype)


def rms_norm(x, w, eps=1e-6):
    xf = x.astype(jnp.float32)
    return (xf * lax.rsqrt(jnp.mean(xf * xf, -1, keepdims=True) + eps) * w.astype(jnp.float32)).astype(x.dtype)


def l2norm(x, eps=1e-6):
    return x * lax.rsqrt(jnp.sum(x * x, -1, keepdims=True) + eps)


def causal_short_conv(x, w):
    k = w.shape[0]
    s = x.shape[1]
    xp = jnp.pad(x, ((0, 0), (k - 1, 0), (0, 0)))
    return sum(xp[:, j:j + s] * w[j] for j in range(k))


def gated_delta_rule_chunked(q, k, v, g, beta):
    b_, h_, s_, dk = q.shape
    dv = v.shape[-1]
    n = s_ // CHUNK
    q = q * (dk ** -0.5)
    k_beta = k * beta[..., None]
    v_beta = v * beta[..., None]
    ch = lambda t: t.reshape(b_, h_, n, CHUNK, t.shape[-1])
    q, k, k_beta, v_beta = ch(q), ch(k), ch(k_beta), ch(v_beta)
    g = jnp.cumsum(g.reshape(b_, h_, n, CHUNK), -1)
    tril = jnp.tril(jnp.ones((CHUNK, CHUNK), bool))
    strict = jnp.tril(jnp.ones((CHUNK, CHUNK), bool), -1)
    gdiff = g[..., :, None] - g[..., None, :]
    decay = jnp.where(tril, jnp.exp(jnp.where(tril, gdiff, 0.0)), 0.0)
    low = jnp.where(strict, jnp.einsum('bhncd,bhnjd->bhncj', k_beta, k) * decay, 0.0)
    a_mat = jnp.eye(CHUNK, dtype=q.dtype) + low
    rhs = jnp.concatenate([v_beta, k_beta * jnp.exp(g)[..., None]], -1)
    sol = lax.linalg.triangular_solve(a_mat, rhs, left_side=True, lower=True, unit_diagonal=True)
    u, w = sol[..., :dv], sol[..., dv:]
    intra = jnp.where(tril, jnp.einsum('bhncd,bhnjd->bhncj', q, k) * decay, 0.0)

    def step(state, inp):
        q_c, k_c, u_c, w_c, intra_c, g_c = inp
        v_new = u_c - jnp.einsum('bhcd,bhde->bhce', w_c, state)
        o = (jnp.einsum('bhcd,bhde->bhce', q_c * jnp.exp(g_c)[..., None], state)
             + jnp.einsum('bhcj,bhje->bhce', intra_c, v_new))
        g_last = g_c[..., -1]
        k_dec = k_c * jnp.exp(g_last[..., None] - g_c)[..., None]
        state = state * jnp.exp(g_last)[..., None, None] + jnp.einsum('bhcd,bhce->bhde', k_dec, v_new)
        return state, o

    xs = tuple(jnp.moveaxis(t, 2, 0) for t in (q, k, u, w, intra, g))
    _, o = lax.scan(step, jnp.zeros((b_, h_, dk, dv), q.dtype), xs)
    return jnp.moveaxis(o, 0, 2).reshape(b_, h_, s_, dv)


def differential_attention(q1, q2, k1, k2, v, lam):
    b_, h_, s_, d = q1.shape
    scale = d ** -0.5
    k_pos = jnp.arange(s_)
    neg = jnp.finfo(jnp.float32).min

    def block(i):
        start = i * Q_BLOCK
        q_pos = start + jnp.arange(Q_BLOCK)
        mask = k_pos[None, :] <= q_pos[:, None]

        def smax(qq, kk):
            qb = lax.dynamic_slice_in_dim(qq, start, Q_BLOCK, axis=2)
            sc = jnp.einsum('bhqd,bhkd->bhqk', qb, kk).astype(jnp.float32) * scale
            return jax.nn.softmax(jnp.where(mask, sc, neg), axis=-1)

        p = smax(q1, k1) - lam * smax(q2, k2)
        return jnp.einsum('bhqk,bhke->bhqe', p.astype(v.dtype), v)

    out = lax.map(block, jnp.arange(s_ // Q_BLOCK))
    return jnp.moveaxis(out, 0, 2).reshape(b_, h_, s_, v.shape[-1])


def setup_inputs(seed: int = 0) -> dict:
    key = jax.random.key(seed)
    ks = jax.random.split(key, 20)
    f32 = jnp.float32
    nrm = lambda k, shape, s: jax.random.normal(k, shape, f32) * s
    x = jax.random.normal(ks[0], (BATCH, SEQ, D_MODEL), f32)
    col_scale = np.concatenate([np.full((n,), DEEPNORM_BETA if i in VALUE_SLOTS else 1.0, np.float32)
                                for i, n in enumerate(IN_SPLITS)])
    w_in = nrm(ks[1], (DEPTH, D_MODEL, IN_WIDTH), D_MODEL ** -0.5) * jnp.asarray(col_scale)
    conv_w = nrm(ks[2], (DEPTH, CONV_K, 3 * GDN_WIDTH), CONV_K ** -0.5)
    a_log = jnp.log(jax.random.uniform(ks[3], (DEPTH, GDN_HEADS), f32, 1.0, 16.0))
    dt = jnp.exp(jax.random.uniform(ks[4], (DEPTH, GDN_HEADS), f32, math.log(1e-3), math.log(1e-1)))
    dt_bias = dt + jnp.log(-jnp.expm1(-dt))
    gdn_norm_w = 1.0 + nrm(ks[5], (DEPTH, GDN_HEAD_DIM), 0.02)
    lambda_q1 = nrm(ks[6], (DEPTH, DIFF_HEAD_DIM), 0.1)
    lambda_k1 = nrm(ks[7], (DEPTH, DIFF_HEAD_DIM), 0.1)
    lambda_q2 = nrm(ks[8], (DEPTH, DIFF_HEAD_DIM), 0.1)
    lambda_k2 = nrm(ks[9], (DEPTH, DIFF_HEAD_DIM), 0.1)
    diff_norm_w = 1.0 + nrm(ks[10], (DEPTH, DIFF_V_DIM), 0.02)
    w_out = nrm(ks[11], (DEPTH, MIX_WIDTH, D_MODEL), MIX_WIDTH ** -0.5 * DEEPNORM_BETA)
    ln1_g = 1.0 + nrm(ks[12], (DEPTH, D_MODEL), 0.02)
    ln1_b = nrm(ks[13], (DEPTH, D_MODEL), 0.02)
    w_gate = nrm(ks[14], (DEPTH, D_MODEL, D_FF), D_MODEL ** -0.5)
    w_up = nrm(ks[15], (DEPTH, D_MODEL, D_FF), D_MODEL ** -0.5)
    w_down = nrm(ks[16], (DEPTH, D_FF, D_MODEL), D_FF ** -0.5 * DEEPNORM_BETA)
    ln2_g = 1.0 + nrm(ks[17], (DEPTH, D_MODEL), 0.02)
    ln2_b = nrm(ks[18], (DEPTH, D_MODEL), 0.02)
    return {"x": x, "w_in": w_in, "conv_w": conv_w, "a_log": a_log, "dt_bias": dt_bias,
            "gdn_norm_w": gdn_norm_w, "lambda_q1": lambda_q1, "lambda_k1": lambda_k1,
            "lambda_q2": lambda_q2, "lambda_k2": lambda_k2, "diff_norm_w": diff_norm_w,
            "w_out": w_out, "ln1_g": ln1_g, "ln1_b": ln1_b, "w_gate": w_gate, "w_up": w_up,
            "w_down": w_down, "ln2_g": ln2_g, "ln2_b": ln2_b}


def reference(x, w_in, conv_w, a_log, dt_bias, gdn_norm_w, lambda_q1, lambda_k1, lambda_q2,
              lambda_k2, diff_norm_w, w_out, ln1_g, ln1_b, w_gate, w_up, w_down, ln2_g, ln2_b):
    b_, s_, _ = x.shape
    split_idx = list(np.cumsum(IN_SPLITS)[:-1])
    heads = lambda t, h: t.reshape(b_, s_, h, -1).transpose(0, 2, 1, 3)
    for l in range(DEPTH):
        lambda_init = 0.8 - 0.6 * math.exp(-0.3 * l)
        proj = jnp.einsum('bsd,de->bse', x, w_in[l])
        gq, gk, gv, gz, ga, gb, dq, dk, dv = jnp.split(proj, split_idx, axis=-1)

        qkv = jax.nn.silu(causal_short_conv(jnp.concatenate([gq, gk, gv], -1), conv_w[l]))
        gq, gk, gv = jnp.split(qkv, 3, axis=-1)
        f32 = jnp.float32
        q_g = l2norm(heads(gq, GDN_HEADS).astype(f32))
        k_g = l2norm(heads(gk, GDN_HEADS).astype(f32))
        v_g = heads(gv, GDN_HEADS).astype(f32)
        beta = jax.nn.sigmoid(gb.astype(f32)).transpose(0, 2, 1)
        g_log = (-jnp.exp(a_log[l].astype(f32))
                 * jax.nn.softplus(ga.astype(f32) + dt_bias[l].astype(f32))).transpose(0, 2, 1)
        o_g = gated_delta_rule_chunked(q_g, k_g, v_g, g_log, beta).astype(x.dtype)
        o_g = o_g.transpose(0, 2, 1, 3)
        o_g = rms_norm(o_g, gdn_norm_w[l]) * jax.nn.silu(gz.reshape(b_, s_, GDN_HEADS, GDN_HEAD_DIM))
        gdn_out = o_g.reshape(b_, s_, GDN_WIDTH)

        dq = dq.reshape(b_, s_, DIFF_HEADS, 2, DIFF_HEAD_DIM).transpose(0, 2, 3, 1, 4)
        dk = dk.reshape(b_, s_, DIFF_HEADS, 2, DIFF_HEAD_DIM).transpose(0, 2, 3, 1, 4)
        v_d = heads(dv, DIFF_HEADS)
        lam = (jnp.exp(jnp.sum(lambda_q1[l].astype(f32) * lambda_k1[l].astype(f32)))
               - jnp.exp(jnp.sum(lambda_q2[l].astype(f32) * lambda_k2[l].astype(f32))) + lambda_init)
        o_d = differential_attention(dq[:, :, 0], dq[:, :, 1], dk[:, :, 0], dk[:, :, 1], v_d, lam)
        o_d = rms_norm(o_d, diff_norm_w[l]) * (1.0 - lambda_init)
        diff_out = o_d.transpose(0, 2, 1, 3).reshape(b_, s_, DIFF_WIDTH)

        mix = jnp.einsum('bse,ed->bsd', jnp.concatenate([gdn_out, diff_out], -1), w_out[l])
        x = layer_norm(DEEPNORM_ALPHA * x + mix, ln1_g[l], ln1_b[l])

        hid = jax.nn.silu(jnp.einsum('bsd,df->bsf', x, w_gate[l])) * jnp.einsum('bsd,df->bsf', x, w_up[l])
        ffn = jnp.einsum('bsf,fd->bsd', hid, w_down[l])
        x = layer_norm(DEEPNORM_ALPHA * x + ffn, ln2_g[l], ln2_b[l])
    return x
```

```python
import functools
import math

import jax
import jax.numpy as jnp
from jax import lax
from jax.experimental import pallas as pl
from jax.experimental.pallas import tpu as pltpu

F32 = jnp.float32
BF16 = jnp.bfloat16

GDN_HEAD_DIM = 128
DIFF_HEAD_DIM = 128
DIFF_V_DIM = 256
CONV_K = 4
CHUNK = 64
LANES = 128
VMEM_LIMIT = 60 * 1024 * 1024


def _dot(a, b):
    return jnp.dot(a.astype(BF16), b.astype(BF16), preferred_element_type=F32)


def _dot_nt(a, b):
    return lax.dot_general(a.astype(BF16), b.astype(BF16), (((1,), (1,)), ((), ())),
                           preferred_element_type=F32)


def _dot_tn(a, b):
    return lax.dot_general(a.astype(BF16), b.astype(BF16), (((0,), (0,)), ((), ())),
                           preferred_element_type=F32)


def _silu(x):
    return x * jax.nn.sigmoid(x)


def _mm_kernel(x_ref, w_ref, o_ref):
    o_ref[...] = jnp.dot(x_ref[...], w_ref[...], preferred_element_type=F32).astype(o_ref.dtype)


def _matmul(x, w, bm, bn, out_dtype, name):
    m, k = x.shape
    n = w.shape[1]
    assert m % bm == 0 and n % bn == 0
    return pl.pallas_call(
        _mm_kernel,
        grid=(m // bm, n // bn),
        in_specs=[pl.BlockSpec((bm, k), lambda i, j: (i, 0)),
                  pl.BlockSpec((k, bn), lambda i, j: (0, j))],
        out_specs=pl.BlockSpec((bm, bn), lambda i, j: (i, j)),
        out_shape=jax.ShapeDtypeStruct((m, n), out_dtype),
        compiler_params=pltpu.CompilerParams(
            dimension_semantics=("parallel", "parallel"), vmem_limit_bytes=VMEM_LIMIT),
        name=name,
    )(x, w)


def _gdn_kernel(q_ref, k_ref, v_ref, z_ref, cwq_ref, cwk_ref, cwv_ref, ga_ref, gb_ref,
                alog_ref, dtb_ref, nw_ref, o_ref,
                xpad, qs, ks, vs, us, ws, qgs, kds, ins, gcs, bts, egs, *, heads, seq):
    d = GDN_HEAD_DIM
    c_ = CHUNK
    nc = seq // c_
    row = lax.broadcasted_iota(jnp.int32, (c_, c_), 0)
    col = lax.broadcasted_iota(jnp.int32, (c_, c_), 1)
    tril = col <= row
    strict = col < row
    eye = col == row
    lane = lax.broadcasted_iota(jnp.int32, (nc, LANES), 1)

    rb = min(seq, 512)

    def conv_silu(x_ref, cw_ref, dst, g, post):
        xpad[0:8, :] = jnp.zeros((8, d), F32)
        xpad[8:8 + seq, :] = x_ref[:, g * d:(g + 1) * d]
        w = cw_ref[:, g * d:(g + 1) * d]
        for r0 in range(0, seq, rb):
            acc = None
            for j in range(CONV_K):
                off = r0 + 8 - (CONV_K - 1) + j
                t = xpad[off:off + rb, :] * w[j:j + 1, :]
                acc = t if acc is None else acc + t
            dst[g, r0:r0 + rb, :] = post(_silu(acc))

    l2n = lambda y: y * lax.rsqrt(jnp.sum(y * y, -1, keepdims=True) + 1e-6)

    for g in range(heads):
        conv_silu(q_ref, cwq_ref, qs, g, lambda y: l2n(y) * (d ** -0.5))
        conv_silu(k_ref, cwk_ref, ks, g, l2n)
        conv_silu(v_ref, cwv_ref, vs, g, lambda y: y)

        a = ga_ref[0, g] + dtb_ref[g]
        sp = jnp.maximum(a, 0.0) + jnp.log1p(jnp.exp(-jnp.abs(a)))
        gl = jnp.where(lane < c_, -jnp.exp(alog_ref[g]) * sp, 0.0)
        sh = 1
        while sh < c_:
            gl = gl + jnp.where(lane >= sh, pltpu.roll(gl, sh, 1), 0.0)
            sh *= 2
        gcs[g] = gl
        bts[g] = jax.nn.sigmoid(gb_ref[0, g])

    def phase1(c, carry):
        r0 = pl.multiple_of(c * c_, c_)
        for g in range(heads):
            qc = qs[g, pl.ds(r0, c_), :]
            kc = ks[g, pl.ds(r0, c_), :]
            vc = vs[g, pl.ds(r0, c_), :]
            gcr = gcs[g, pl.ds(c, 1), :][:, :c_]
            btr = bts[g, pl.ds(c, 1), :][:, :c_]
            gcol = jnp.sum(jnp.where(eye, jnp.broadcast_to(gcr, (c_, c_)), 0.0), 1, keepdims=True)
            bcol = jnp.sum(jnp.where(eye, jnp.broadcast_to(btr, (c_, c_)), 0.0), 1, keepdims=True)
            glast = gcr[:, c_ - 1:c_]
            decay = jnp.where(tril, jnp.exp(jnp.where(tril, gcol - gcr, 0.0)), 0.0)
            kb = kc * bcol
            nl = jnp.where(strict, -(_dot_nt(kb, kc) * decay), 0.0)
            r = jnp.concatenate([vc * bcol, kb * jnp.exp(gcol)], axis=1)
            p = 1
            while True:
                r = r + _dot(nl, r)
                p *= 2
                if p >= c_:
                    break
                nl = _dot(nl, nl)
            us[g, pl.ds(r0, c_), :] = r[:, :d]
            ws[g, pl.ds(r0, c_), :] = r[:, d:].astype(BF16)
            ins[g, pl.ds(r0, c_), :] = jnp.where(tril, _dot_nt(qc, kc) * decay, 0.0).astype(BF16)
            qgs[g, pl.ds(r0, c_), :] = (qc * jnp.exp(gcol)).astype(BF16)
            kds[g, pl.ds(r0, c_), :] = (kc * jnp.exp(glast - gcol)).astype(BF16)
            egs[g, pl.ds(c, 1), :] = jnp.broadcast_to(jnp.exp(glast), (1, LANES))
        return carry

    lax.fori_loop(0, nc, phase1, 0)

    def phase2(c, states):
        r0 = pl.multiple_of(c * c_, c_)
        new_states = []
        for g in range(heads):
            st = states[g]
            v_new = us[g, pl.ds(r0, c_), :] - _dot(ws[g, pl.ds(r0, c_), :], st)
            o = _dot(qgs[g, pl.ds(r0, c_), :], st) + _dot(ins[g, pl.ds(r0, c_), :], v_new)
            st = st * egs[g, pl.ds(c, 1), :] + _dot_tn(kds[g, pl.ds(r0, c_), :], v_new)
            new_states.append(st)
            on = o * lax.rsqrt(jnp.mean(o * o, -1, keepdims=True) + 1e-6) * nw_ref[...]
            z = z_ref[pl.ds(r0, c_), g * d:(g + 1) * d]
            o_ref[pl.ds(r0, c_), g * d:(g + 1) * d] = (on * _silu(z)).astype(o_ref.dtype)
        return tuple(new_states)

    lax.fori_loop(0, nc, phase2, tuple(jnp.zeros((d, d), F32) for _ in range(heads)))


def _gdn(proj, conv_w, ga_rows, gb_rows, a_log, dt_bias, norm_w, *, batch, seq, n_heads, heads):
    d = GDN_HEAD_DIM
    gw = heads * d
    hb = n_heads // heads
    nc = seq // CHUNK
    m = batch * seq
    kern = functools.partial(_gdn_kernel, heads=heads, seq=seq)
    col = lambda off: (lambda b, h: (b, off * hb + h))
    cw = lambda off: (lambda b, h: (0, off * hb + h))
    return pl.pallas_call(
        kern,
        grid=(batch, hb),
        in_specs=[
            pl.BlockSpec((seq, gw), col(0)), pl.BlockSpec((seq, gw), col(1)),
            pl.BlockSpec((seq, gw), col(2)), pl.BlockSpec((seq, gw), col(3)),
            pl.BlockSpec((CONV_K, gw), cw(0)), pl.BlockSpec((CONV_K, gw), cw(1)),
            pl.BlockSpec((CONV_K, gw), cw(2)),
            pl.BlockSpec((1, heads, nc, LANES), lambda b, h: (b, h, 0, 0)),
            pl.BlockSpec((1, heads, nc, LANES), lambda b, h: (b, h, 0, 0)),
            pl.BlockSpec((heads, 1, 1), lambda b, h: (h, 0, 0)),
            pl.BlockSpec((heads, 1, 1), lambda b, h: (h, 0, 0)),
            pl.BlockSpec((1, d), lambda b, h: (0, 0)),
        ],
        out_specs=pl.BlockSpec((seq, gw), lambda b, h: (b, h)),
        out_shape=jax.ShapeDtypeStruct((m, n_heads * d), BF16),
        scratch_shapes=[
            pltpu.VMEM((seq + 8, d), F32),
            pltpu.VMEM((heads, seq, d), F32),
            pltpu.VMEM((heads, seq, d), F32),
            pltpu.VMEM((heads, seq, d), F32),
            pltpu.VMEM((heads, seq, d), F32),
            pltpu.VMEM((heads, seq, d), BF16),
            pltpu.VMEM((heads, seq, d), BF16),
            pltpu.VMEM((heads, seq, d), BF16),
            pltpu.VMEM((heads, seq, CHUNK), BF16),
            pltpu.VMEM((heads, nc, LANES), F32),
            pltpu.VMEM((heads, nc, LANES), F32),
            pltpu.VMEM((heads, nc, LANES), F32),
        ],
        compiler_params=pltpu.CompilerParams(
            dimension_semantics=("parallel", "parallel"), vmem_limit_bytes=VMEM_LIMIT),
        name="gdn",
    )(proj, proj, proj, proj, conv_w, conv_w, conv_w, ga_rows, gb_rows,
      a_log.reshape(n_heads, 1, 1), dt_bias.reshape(n_heads, 1, 1), norm_w.reshape(1, d))


def _diff_attn_kernel(q_ref, k_ref, v_ref, lq1_ref, lk1_ref, lq2_ref, lk2_ref, nw_ref, o_ref,
                      acc1, acc2, *, tq, lambda_init):
    dh = DIFF_HEAD_DIM
    qi = pl.program_id(2)
    scale = dh ** -0.5
    neg = float(jnp.finfo(jnp.float32).min)
    q1 = q_ref[:, :dh]
    q2 = q_ref[:, dh:]
    acc1[...] = jnp.zeros_like(acc1)
    acc2[...] = jnp.zeros_like(acc2)

    def scores(j):
        r0 = pl.multiple_of(j * tq, tq)
        kj = k_ref[pl.ds(r0, tq), :]
        s1 = _dot_nt(q1, kj[:, :dh]) * scale
        s2 = _dot_nt(q2, kj[:, dh:]) * scale
        return s1, s2, v_ref[pl.ds(r0, tq), :]

    def update(s, vj, m, l, acc):
        m_new = jnp.maximum(m, jnp.max(s, -1, keepdims=True))
        alpha = jnp.exp(m - m_new)
        p = jnp.exp(s - m_new)
        l_new = alpha * l + jnp.sum(p, -1, keepdims=True)
        acc[...] = acc[...] * alpha + _dot(p, vj)
        return m_new, l_new

    def body(j, carry):
        m1, l1, m2, l2 = carry
        s1, s2, vj = scores(j)
        m1, l1 = update(s1, vj, m1, l1, acc1)
        m2, l2 = update(s2, vj, m2, l2, acc2)
        return m1, l1, m2, l2

    init = (jnp.full((tq, 1), neg, F32), jnp.zeros((tq, 1), F32),
            jnp.full((tq, 1), neg, F32), jnp.zeros((tq, 1), F32))
    m1, l1, m2, l2 = lax.fori_loop(0, qi, body, init)
    s1, s2, vj = scores(qi)
    row = lax.broadcasted_iota(jnp.int32, (tq, tq), 0)
    col = lax.broadcasted_iota(jnp.int32, (tq, tq), 1)
    keep = col <= row
    m1, l1 = update(jnp.where(keep, s1, neg), vj, m1, l1, acc1)
    m2, l2 = update(jnp.where(keep, s2, neg), vj, m2, l2, acc2)

    lam = (jnp.exp(jnp.sum(lq1_ref[...] * lk1_ref[...], -1, keepdims=True))
           - jnp.exp(jnp.sum(lq2_ref[...] * lk2_ref[...], -1, keepdims=True)) + lambda_init)
    o = acc1[...] / l1 - lam * (acc2[...] / l2)
    on = o * lax.rsqrt(jnp.mean(o * o, -1, keepdims=True) + 1e-6) * nw_ref[...]
    o_ref[...] = (on * (1.0 - lambda_init)).astype(o_ref.dtype)


def _diff_attn(proj_d, lq1, lk1, lq2, lk2, norm_w, *, batch, seq, n_heads, tq, lambda_init):
    dv = DIFF_V_DIM
    m = batch * seq
    nq = seq // tq
    kern = functools.partial(_diff_attn_kernel, tq=tq, lambda_init=lambda_init)
    vec = lambda n: pl.BlockSpec((1, n), lambda b, h, i: (0, 0))
    return pl.pallas_call(
        kern,
        grid=(batch, n_heads, nq),
        in_specs=[
            pl.BlockSpec((tq, dv), lambda b, h, i: (b * nq + i, h)),
            pl.BlockSpec((seq, dv), lambda b, h, i: (b, n_heads + h)),
            pl.BlockSpec((seq, dv), lambda b, h, i: (b, 2 * n_heads + h)),
            vec(DIFF_HEAD_DIM), vec(DIFF_HEAD_DIM), vec(DIFF_HEAD_DIM), vec(DIFF_HEAD_DIM), vec(dv),
        ],
        out_specs=pl.BlockSpec((tq, dv), lambda b, h, i: (b * nq + i, h)),
        out_shape=jax.ShapeDtypeStruct((m, n_heads * dv), BF16),
        scratch_shapes=[pltpu.VMEM((tq, dv), F32), pltpu.VMEM((tq, dv), F32)],
        compiler_params=pltpu.CompilerParams(
            dimension_semantics=("parallel", "parallel", "parallel"), vmem_limit_bytes=VMEM_LIMIT),
        name="diff_attn",
    )(proj_d, proj_d, proj_d, lq1.reshape(1, -1), lk1.reshape(1, -1), lq2.reshape(1, -1),
      lk2.reshape(1, -1), norm_w.reshape(1, -1))


def _layer_norm(y, g, b):
    mu = jnp.mean(y, -1, keepdims=True)
    yc = y - mu
    var = jnp.mean(yc * yc, -1, keepdims=True)
    return yc * lax.rsqrt(var + 1e-5) * g + b


LN_ROWS = 64


def _deepnorm_epilogue(x_ref, acc_ref, g_ref, beta_ref, alpha, out_refs):
    g = g_ref[...]
    beta = beta_ref[...]

    def rows(i, carry):
        r0 = pl.multiple_of(i * LN_ROWS, LN_ROWS)
        sl = pl.ds(r0, LN_ROWS)
        y = _layer_norm(alpha * x_ref[sl, :] + acc_ref[sl, :], g, beta)
        for o in out_refs:
            o[sl, :] = y.astype(o.dtype)
        return carry

    lax.fori_loop(0, x_ref.shape[0] // LN_ROWS, rows, 0)


def _out_proj_kernel(a_ref, b_ref, w_ref, x_ref, g_ref, beta_ref, o_ref, ob_ref, *, nk, alpha):
    k = pl.program_id(1)
    half = nk // 2

    @pl.when(k == 0)
    def _():
        o_ref[...] = jnp.zeros_like(o_ref)

    def accumulate(lhs_ref):
        n = o_ref.shape[1]
        nb = min(n, 512)
        lhs = lhs_ref[...]
        for n0 in range(0, n, nb):
            o_ref[:, n0:n0 + nb] += jnp.dot(lhs, w_ref[:, n0:n0 + nb], preferred_element_type=F32)

    @pl.when(k < half)
    def _():
        accumulate(a_ref)

    @pl.when(k >= half)
    def _():
        accumulate(b_ref)

    @pl.when(k == nk - 1)
    def _():
        _deepnorm_epilogue(x_ref, o_ref, g_ref, beta_ref, alpha, (o_ref, ob_ref))


def _out_proj_ln(a, b, w, x, g, beta, *, bm, bk, alpha):
    m, ka = a.shape
    n = w.shape[1]
    half = ka // bk
    nk = 2 * half
    kern = functools.partial(_out_proj_kernel, nk=nk, alpha=alpha)
    return pl.pallas_call(
        kern,
        grid=(m // bm, nk),
        in_specs=[
            pl.BlockSpec((bm, bk), lambda i, k: (i, jnp.minimum(k, half - 1))),
            pl.BlockSpec((bm, bk), lambda i, k: (i, jnp.maximum(k - half, 0))),
            pl.BlockSpec((bk, n), lambda i, k: (k, 0)),
            pl.BlockSpec((bm, n), lambda i, k: (i, 0)),
            pl.BlockSpec((1, n), lambda i, k: (0, 0)),
            pl.BlockSpec((1, n), lambda i, k: (0, 0)),
        ],
        out_specs=[pl.BlockSpec((bm, n), lambda i, k: (i, 0)),
                   pl.BlockSpec((bm, n), lambda i, k: (i, 0))],
        out_shape=[jax.ShapeDtypeStruct((m, n), F32), jax.ShapeDtypeStruct((m, n), BF16)],
        compiler_params=pltpu.CompilerParams(
            dimension_semantics=("parallel", "arbitrary"), vmem_limit_bytes=VMEM_LIMIT),
        name="out_proj_ln",
    )(a, b, w, x, g.reshape(1, n), beta.reshape(1, n))


def _ffn_kernel(xb_ref, wg_ref, wu_ref, wd_ref, x_ref, g_ref, beta_ref, o_ref, *, nf, alpha):
    f = pl.program_id(1)

    @pl.when(f == 0)
    def _():
        o_ref[...] = jnp.zeros_like(o_ref)

    xb = xb_ref[...]
    gate = jnp.dot(xb, wg_ref[...], preferred_element_type=F32)
    up = jnp.dot(xb, wu_ref[...], preferred_element_type=F32)
    hid = (_silu(gate) * up).astype(BF16)
    dm = o_ref.shape[1]
    nb = min(dm, 512)
    for n0 in range(0, dm, nb):
        o_ref[:, n0:n0 + nb] += jnp.dot(hid, wd_ref[:, n0:n0 + nb], preferred_element_type=F32)

    @pl.when(f == nf - 1)
    def _():
        _deepnorm_epilogue(x_ref, o_ref, g_ref, beta_ref, alpha, (o_ref,))


def _ffn_ln(xb, x, wg, wu, wd, g, beta, *, bm, bf, alpha):
    m, dm = xb.shape
    ff = wg.shape[1]
    nf = ff // bf
    assert ff % bf == 0
    kern = functools.partial(_ffn_kernel, nf=nf, alpha=alpha)
    return pl.pallas_call(
        kern,
        grid=(m // bm, nf),
        in_specs=[
            pl.BlockSpec((bm, dm), lambda i, f: (i, 0)),
            pl.BlockSpec((dm, bf), lambda i, f: (0, f)),
            pl.BlockSpec((dm, bf), lambda i, f: (0, f)),
            pl.BlockSpec((bf, dm), lambda i, f: (f, 0)),
            pl.BlockSpec((bm, dm), lambda i, f: (i, 0)),
            pl.BlockSpec((1, dm), lambda i, f: (0, 0)),
            pl.BlockSpec((1, dm), lambda i, f: (0, 0)),
        ],
        out_specs=pl.BlockSpec((bm, dm), lambda i, f: (i, 0)),
        out_shape=jax.ShapeDtypeStruct((m, dm), F32),
        compiler_params=pltpu.CompilerParams(
            dimension_semantics=("parallel", "arbitrary"), vmem_limit_bytes=VMEM_LIMIT),
        name="ffn_ln",
    )(xb, wg, wu, wd, x, g.reshape(1, dm), beta.reshape(1, dm))


def _layer(x2, batch, seq, l, w_in, conv_w, a_log, dt_bias, gdn_norm_w, lq1, lk1, lq2, lk2,
           diff_norm_w, w_out, ln1_g, ln1_b, w_gate, w_up, w_down, ln2_g, ln2_b, alpha):
    m, dm = x2.shape
    n_gdn_heads = a_log.shape[0]
    gdn_w = n_gdn_heads * GDN_HEAD_DIM
    diff_w = w_out.shape[0] - gdn_w
    n_diff_heads = diff_w // DIFF_V_DIM
    lambda_init = 0.8 - 0.6 * math.exp(-0.3 * l)
    nc = seq // CHUNK

    bn = 768
    main = 4 * gdn_w
    ab = 2 * n_gdn_heads
    wide = -(-(main + ab) // bn) * bn
    w_a = jnp.concatenate([w_in[:, :main + ab], jnp.zeros((dm, wide - main - ab), w_in.dtype)], 1)
    w_a = w_a.astype(BF16)
    w_b = w_in[:, main + ab:].astype(BF16)
    xb = x2.astype(BF16)

    proj_g = _matmul(xb, w_a, 1024, bn, F32, "proj_gdn")
    proj_d = _matmul(xb, w_b, 1024, bn, BF16, "proj_diff")

    gates = proj_g[:, main:main + ab].reshape(batch, seq, ab).transpose(0, 2, 1)
    gates = gates.reshape(batch, ab, nc, CHUNK)
    gates = jnp.pad(gates, ((0, 0), (0, 0), (0, 0), (0, LANES - CHUNK)))
    gdn_out = _gdn(proj_g, conv_w, gates[:, :n_gdn_heads], gates[:, n_gdn_heads:], a_log, dt_bias,
                   gdn_norm_w, batch=batch, seq=seq, n_heads=n_gdn_heads, heads=2)

    diff_out = _diff_attn(proj_d, lq1, lk1, lq2, lk2, diff_norm_w, batch=batch, seq=seq,
                          n_heads=n_diff_heads, tq=256, lambda_init=lambda_init)

    x1, x1b = _out_proj_ln(gdn_out, diff_out, w_out.astype(BF16), x2, ln1_g, ln1_b,
                           bm=256, bk=1024, alpha=alpha)
    return _ffn_ln(x1b, x1, w_gate.astype(BF16), w_up.astype(BF16), w_down.astype(BF16),
                   ln2_g, ln2_b, bm=512, bf=256, alpha=alpha)


def kernel(x, w_in, conv_w, a_log, dt_bias, gdn_norm_w, lambda_q1, lambda_k1, lambda_q2, lambda_k2,
           diff_norm_w, w_out, ln1_g, ln1_b, w_gate, w_up, w_down, ln2_g, ln2_b):
    batch, seq, dm = x.shape
    depth = w_in.shape[0]
    alpha = (2 * depth) ** 0.25
    x2 = x.reshape(batch * seq, dm)
    for l in range(depth):
        x2 = _layer(x2, batch, seq, l, w_in[l], conv_w[l], a_log[l], dt_bias[l], gdn_norm_w[l],
                    lambda_q1[l], lambda_k1[l], lambda_q2[l], lambda_k2[l], diff_norm_w[l], w_out[l],
                    ln1_g[l], ln1_b[l], w_gate[l], w_up[l], w_down[l], ln2_g[l], ln2_b[l], alpha)
    return x2.reshape(batch, seq, dm)
```

```python
import functools
import math

import jax
import jax.numpy as jnp
from jax import lax
from jax.experimental import pallas as pl
from jax.experimental.pallas import tpu as pltpu

F32 = jnp.float32
BF16 = jnp.bfloat16

GDN_HEAD_DIM = 128
DIFF_HEAD_DIM = 128
DIFF_V_DIM = 256
CONV_K = 4
CHUNK = 64
LANES = 128
VMEM_LIMIT = 60 * 1024 * 1024


def _dot(a, b):
    return jnp.dot(a.astype(BF16), b.astype(BF16), preferred_element_type=F32)


def _dot_nt(a, b):
    return lax.dot_general(a.astype(BF16), b.astype(BF16), (((1,), (1,)), ((), ())),
                           preferred_element_type=F32)


def _dot_tn(a, b):
    return lax.dot_general(a.astype(BF16), b.astype(BF16), (((0,), (0,)), ((), ())),
                           preferred_element_type=F32)


def _silu(x):
    return x * jax.nn.sigmoid(x)


def _mm_kernel(x_ref, w_ref, o_ref):
    o_ref[...] = jnp.dot(x_ref[...], w_ref[...], preferred_element_type=F32).astype(o_ref.dtype)


def _matmul(x, w, bm, bn, out_dtype, name):
    m, k = x.shape
    n = w.shape[1]
    assert m % bm == 0 and n % bn == 0
    return pl.pallas_call(
        _mm_kernel,
        grid=(m // bm, n // bn),
        in_specs=[pl.BlockSpec((bm, k), lambda i, j: (i, 0)),
                  pl.BlockSpec((k, bn), lambda i, j: (0, j))],
        out_specs=pl.BlockSpec((bm, bn), lambda i, j: (i, j)),
        out_shape=jax.ShapeDtypeStruct((m, n), out_dtype),
        compiler_params=pltpu.CompilerParams(
            dimension_semantics=("parallel", "parallel"), vmem_limit_bytes=VMEM_LIMIT),
        name=name,
    )(x, w)


CAST_ROWS = 128


def _mm_wt32_kernel(x_ref, w_ref, o_ref, wb_ref):
    @pl.when(pl.program_id(1) == 0)
    def _():
        for r in range(0, w_ref.shape[0], CAST_ROWS):
            wb_ref[r:r + CAST_ROWS, :] = w_ref[r:r + CAST_ROWS, :].astype(BF16)

    o_ref[...] = _dot_nt(x_ref[...], wb_ref[...]).astype(o_ref.dtype)


def _matmul_wt32(x, w_t, row0, n_out, bm, bn, out_dtype, name):
    m, k = x.shape
    assert m % bm == 0 and n_out % bn == 0 and bn % CAST_ROWS == 0 and row0 % 8 == 0
    assert row0 + n_out <= w_t.shape[0] and k == w_t.shape[1]
    return pl.pallas_call(
        _mm_wt32_kernel,
        grid=(n_out // bn, m // bm),
        in_specs=[pl.BlockSpec((bm, k), lambda j, i: (i, 0)),
                  pl.BlockSpec((pl.Element(bn), pl.Element(k)), lambda j, i: ((row0 // 8 + j * (bn // 8)) * 8, 0))],
        out_specs=pl.BlockSpec((bm, bn), lambda j, i: (i, j)),
        out_shape=jax.ShapeDtypeStruct((m, n_out), out_dtype),
        scratch_shapes=[pltpu.VMEM((bn, k), BF16)],
        compiler_params=pltpu.CompilerParams(
            dimension_semantics=("parallel", "arbitrary"), vmem_limit_bytes=VMEM_LIMIT),
        name=name,
    )(x, w_t)


def _gdn_kernel(q_ref, k_ref, v_ref, z_ref, cwq_ref, cwk_ref, cwv_ref, ga_ref, gb_ref,
                alog_ref, dtb_ref, nw_ref, o_ref,
                xpad, qs, ks, vs, q2s, k2s, o0s, rs, gcs, bts, egs, *, heads, seq, unroll1, unroll2):
    d = GDN_HEAD_DIM
    c_ = CHUNK
    nc = seq // c_
    row = lax.broadcasted_iota(jnp.int32, (c_, c_), 0)
    col = lax.broadcasted_iota(jnp.int32, (c_, c_), 1)
    tril = col <= row
    strict = col < row
    eye = col == row
    lane = lax.broadcasted_iota(jnp.int32, (nc, LANES), 1)

    rb = min(seq, 512)

    def conv_silu(x_ref, cw_ref, dst, g, post):
        xpad[0:8, :] = jnp.zeros((8, d), F32)
        xpad[8:8 + seq, :] = x_ref[:, g * d:(g + 1) * d]
        w = cw_ref[:, g * d:(g + 1) * d]
        for r0 in range(0, seq, rb):
            acc = None
            for j in range(CONV_K):
                off = r0 + 8 - (CONV_K - 1) + j
                t = xpad[off:off + rb, :] * w[j:j + 1, :]
                acc = t if acc is None else acc + t
            dst[g, r0:r0 + rb, :] = post(_silu(acc))

    l2n = lambda y: y * lax.rsqrt(jnp.sum(y * y, -1, keepdims=True) + 1e-6)

    for g in range(heads):
        conv_silu(q_ref, cwq_ref, qs, g, lambda y: l2n(y) * (d ** -0.5))
        conv_silu(k_ref, cwk_ref, ks, g, l2n)
        conv_silu(v_ref, cwv_ref, vs, g, lambda y: y)

        a = ga_ref[0, g] + dtb_ref[g]
        sp = jnp.maximum(a, 0.0) + jnp.log1p(jnp.exp(-jnp.abs(a)))
        gl = jnp.where(lane < c_, -jnp.exp(alog_ref[g]) * sp, 0.0)
        sh = 1
        while sh < c_:
            gl = gl + jnp.where(lane >= sh, pltpu.roll(gl, sh, 1), 0.0)
            sh *= 2
        gcs[g] = gl
        bts[g] = jax.nn.sigmoid(gb_ref[0, g])

    def phase1(cg, carry):
        chains = [(cg * unroll1 + j, g) for j in range(unroll1) for g in range(heads)]
        st = []
        for c, g in chains:
            r0 = pl.multiple_of(c * c_, c_)
            qc = qs[g, pl.ds(r0, c_), :]
            kc = ks[g, pl.ds(r0, c_), :]
            vc = vs[g, pl.ds(r0, c_), :]
            gcr = gcs[g, pl.ds(c, 1), :][:, :c_]
            btr = bts[g, pl.ds(c, 1), :][:, :c_]
            gcol = jnp.sum(jnp.where(eye, jnp.broadcast_to(gcr, (c_, c_)), 0.0), 1, keepdims=True)
            bcol = jnp.sum(jnp.where(eye, jnp.broadcast_to(btr, (c_, c_)), 0.0), 1, keepdims=True)
            glast = gcr[:, c_ - 1:c_]
            decay = jnp.where(tril, jnp.exp(jnp.where(tril, gcol - gcr, 0.0)), 0.0)
            kb = kc * bcol
            r = jnp.concatenate([vc * bcol, kb * jnp.exp(gcol)], axis=1)
            st.append(dict(c=c, g=g, r0=r0, qc=qc, kc=kc, kb=kb, gcol=gcol, glast=glast, decay=decay, r=r))
        for t in st:
            t["aq"] = _dot_nt(jnp.concatenate([t["kb"], t["qc"]], axis=0), t["kc"])
        for t in st:
            t["nl"] = jnp.where(strict, -(t["aq"][:c_] * t["decay"]), 0.0)
            t["intra"] = jnp.where(tril, t["aq"][c_:] * t["decay"], 0.0)
        p = 1
        while True:
            for t in st:
                t["r"] = t["r"] + _dot(t["nl"], t["r"])
            p *= 2
            if p >= c_:
                break
            for t in st:
                t["nl"] = _dot(t["nl"], t["nl"])
        for t in st:
            kd = t["kc"] * jnp.exp(t["glast"] - t["gcol"])
            t["kq"] = _dot_tn(kd, t["r"])
            t["io"] = _dot(t["intra"], t["r"])
        for t in st:
            c, g, r0 = t["c"], t["g"], t["r0"]
            rd = pl.multiple_of(c * d, d)
            q2s[g, pl.ds(rd, d), :] = t["kq"][:, :d]
            k2s[g, pl.ds(rd, d), :] = t["kq"][:, d:].astype(BF16)
            o0s[g, pl.ds(r0, c_), :] = t["io"][:, :d]
            rs[g, pl.ds(r0, c_), :] = (t["qc"] * jnp.exp(t["gcol"]) - t["io"][:, d:]).astype(BF16)
            egs[g, pl.ds(c, 1), :] = jnp.broadcast_to(jnp.exp(t["glast"]), (1, LANES))
        return carry

    lax.fori_loop(0, nc // unroll1, phase1, 0)

    def phase2(c, states):
        r0 = pl.multiple_of(c * c_, c_)
        rd = pl.multiple_of(c * d, d)
        sbs = [states[g].astype(BF16) for g in range(heads)]
        ks2 = [_dot(k2s[g, pl.ds(rd, d), :], sbs[g]) for g in range(heads)]
        ros = [_dot(rs[g, pl.ds(r0, c_), :], sbs[g]) for g in range(heads)]
        new_states = [states[g] * egs[g, pl.ds(c, 1), :] + q2s[g, pl.ds(rd, d), :] - ks2[g]
                      for g in range(heads)]
        for g in range(heads):
            o = o0s[g, pl.ds(r0, c_), :] + ros[g]
            on = o * lax.rsqrt(jnp.mean(o * o, -1, keepdims=True) + 1e-6) * nw_ref[...]
            z = z_ref[pl.ds(r0, c_), g * d:(g + 1) * d]
            o_ref[pl.ds(r0, c_), g * d:(g + 1) * d] = (on * _silu(z)).astype(o_ref.dtype)
        return tuple(new_states)

    lax.fori_loop(0, nc, phase2, tuple(jnp.zeros((d, d), F32) for _ in range(heads)), unroll=unroll2)


def _gdn(proj, conv_w, ga_rows, gb_rows, a_log, dt_bias, norm_w, *, batch, seq, n_heads, heads):
    d = GDN_HEAD_DIM
    gw = heads * d
    hb = n_heads // heads
    nc = seq // CHUNK
    m = batch * seq
    kern = functools.partial(_gdn_kernel, heads=heads, seq=seq, unroll1=4, unroll2=2)
    col = lambda off: (lambda b, h: (b, off * hb + h))
    cw = lambda off: (lambda b, h: (0, off * hb + h))
    return pl.pallas_call(
        kern,
        grid=(batch, hb),
        in_specs=[
            pl.BlockSpec((seq, gw), col(0)), pl.BlockSpec((seq, gw), col(1)),
            pl.BlockSpec((seq, gw), col(2)), pl.BlockSpec((seq, gw), col(3)),
            pl.BlockSpec((CONV_K, gw), cw(0)), pl.BlockSpec((CONV_K, gw), cw(1)),
            pl.BlockSpec((CONV_K, gw), cw(2)),
            pl.BlockSpec((1, heads, nc, LANES), lambda b, h: (b, h, 0, 0)),
            pl.BlockSpec((1, heads, nc, LANES), lambda b, h: (b, h, 0, 0)),
            pl.BlockSpec((heads, 1, 1), lambda b, h: (h, 0, 0)),
            pl.BlockSpec((heads, 1, 1), lambda b, h: (h, 0, 0)),
            pl.BlockSpec((1, d), lambda b, h: (0, 0)),
        ],
        out_specs=pl.BlockSpec((seq, gw), lambda b, h: (b, h)),
        out_shape=jax.ShapeDtypeStruct((m, n_heads * d), BF16),
        scratch_shapes=[
            pltpu.VMEM((seq + 8, d), F32),
            pltpu.VMEM((heads, seq, d), F32),
            pltpu.VMEM((heads, seq, d), F32),
            pltpu.VMEM((heads, seq, d), F32),
            pltpu.VMEM((heads, nc * d, d), F32),
            pltpu.VMEM((heads, nc * d, d), BF16),
            pltpu.VMEM((heads, seq, d), F32),
            pltpu.VMEM((heads, seq, d), BF16),
            pltpu.VMEM((heads, nc, LANES), F32),
            pltpu.VMEM((heads, nc, LANES), F32),
            pltpu.VMEM((heads, nc, LANES), F32),
        ],
        compiler_params=pltpu.CompilerParams(
            dimension_semantics=("parallel", "parallel"), vmem_limit_bytes=VMEM_LIMIT),
        name="gdn",
    )(proj, proj, proj, proj, conv_w, conv_w, conv_w, ga_rows, gb_rows,
      a_log.reshape(n_heads, 1, 1), dt_bias.reshape(n_heads, 1, 1), norm_w.reshape(1, d))


def _diff_attn_kernel(q_ref, k_ref, v_ref, lq1_ref, lk1_ref, lq2_ref, lk2_ref, nw_ref, o_ref,
                      acc1, acc2, *, tq, lambda_init):
    dh = DIFF_HEAD_DIM
    qi = pl.program_id(2)
    c2 = (dh ** -0.5) * math.log2(math.e)
    neg = float(jnp.finfo(jnp.float32).min)
    q1 = q_ref[:, :dh]
    q2 = q_ref[:, dh:]
    acc1[...] = jnp.zeros_like(acc1)
    acc2[...] = jnp.zeros_like(acc2)

    def block(j, carry, masked):
        m1, l1, m2, l2 = carry
        r0 = pl.multiple_of(j * tq, tq)
        kj = k_ref[pl.ds(r0, tq), :]
        vj = v_ref[pl.ds(r0, tq), :]
        s1 = _dot_nt(q1, kj[:, :dh])
        s2 = _dot_nt(q2, kj[:, dh:])
        if masked:
            row = lax.broadcasted_iota(jnp.int32, (tq, tq), 0)
            col = lax.broadcasted_iota(jnp.int32, (tq, tq), 1)
            keep = col <= row
            s1 = jnp.where(keep, s1, neg)
            s2 = jnp.where(keep, s2, neg)
        m1n = jnp.maximum(m1, jnp.max(s1, -1, keepdims=True))
        m2n = jnp.maximum(m2, jnp.max(s2, -1, keepdims=True))
        a1 = jnp.exp2((m1 - m1n) * c2)
        a2 = jnp.exp2((m2 - m2n) * c2)
        p1 = jnp.exp2((s1 - m1n) * c2)
        p2 = jnp.exp2((s2 - m2n) * c2)
        l1 = a1 * l1 + jnp.sum(p1, -1, keepdims=True)
        l2 = a2 * l2 + jnp.sum(p2, -1, keepdims=True)
        pv1 = _dot(p1, vj)
        pv2 = _dot(p2, vj)
        acc1[...] = acc1[...] * a1 + pv1
        acc2[...] = acc2[...] * a2 + pv2
        return m1n, l1, m2n, l2

    init = (jnp.full((tq, 1), neg, F32), jnp.zeros((tq, 1), F32),
            jnp.full((tq, 1), neg, F32), jnp.zeros((tq, 1), F32))
    carry = lax.fori_loop(0, qi, lambda j, c: block(j, c, False), init)
    m1, l1, m2, l2 = block(qi, carry, True)

    lam = (jnp.exp(jnp.sum(lq1_ref[...] * lk1_ref[...], -1, keepdims=True))
           - jnp.exp(jnp.sum(lq2_ref[...] * lk2_ref[...], -1, keepdims=True)) + lambda_init)
    o = acc1[...] / l1 - lam * (acc2[...] / l2)
    on = o * lax.rsqrt(jnp.mean(o * o, -1, keepdims=True) + 1e-6) * nw_ref[...]
    o_ref[...] = (on * (1.0 - lambda_init)).astype(o_ref.dtype)


def _diff_attn(proj_d, lq1, lk1, lq2, lk2, norm_w, *, batch, seq, n_heads, tq, lambda_init):
    dv = DIFF_V_DIM
    m = batch * seq
    nq = seq // tq
    kern = functools.partial(_diff_attn_kernel, tq=tq, lambda_init=lambda_init)
    vec = lambda n: pl.BlockSpec((1, n), lambda b, h, i: (0, 0))
    return pl.pallas_call(
        kern,
        grid=(batch, n_heads, nq),
        in_specs=[
            pl.BlockSpec((tq, dv), lambda b, h, i: (b * nq + i, h)),
            pl.BlockSpec((seq, dv), lambda b, h, i: (b, n_heads + h)),
            pl.BlockSpec((seq, dv), lambda b, h, i: (b, 2 * n_heads + h)),
            vec(DIFF_HEAD_DIM), vec(DIFF_HEAD_DIM), vec(DIFF_HEAD_DIM), vec(DIFF_HEAD_DIM), vec(dv),
        ],
        out_specs=pl.BlockSpec((tq, dv), lambda b, h, i: (b * nq + i, h)),
        out_shape=jax.ShapeDtypeStruct((m, n_heads * dv), BF16),
        scratch_shapes=[pltpu.VMEM((tq, dv), F32), pltpu.VMEM((tq, dv), F32)],
        compiler_params=pltpu.CompilerParams(
            dimension_semantics=("parallel", "parallel", "parallel"), vmem_limit_bytes=VMEM_LIMIT),
        name="diff_attn",
    )(proj_d, proj_d, proj_d, lq1.reshape(1, -1), lk1.reshape(1, -1), lq2.reshape(1, -1),
      lk2.reshape(1, -1), norm_w.reshape(1, -1))


def _layer_norm(y, g, b):
    mu = jnp.mean(y, -1, keepdims=True)
    yc = y - mu
    var = jnp.mean(yc * yc, -1, keepdims=True)
    return yc * lax.rsqrt(var + 1e-5) * g + b


LN_ROWS = 64


def _deepnorm_epilogue(x_ref, acc_ref, g_ref, beta_ref, alpha, out_refs):
    g = g_ref[...]
    beta = beta_ref[...]

    def rows(i, carry):
        r0 = pl.multiple_of(i * LN_ROWS, LN_ROWS)
        sl = pl.ds(r0, LN_ROWS)
        y = _layer_norm(alpha * x_ref[sl, :] + acc_ref[sl, :], g, beta)
        for o in out_refs:
            o[sl, :] = y.astype(o.dtype)
        return carry

    lax.fori_loop(0, x_ref.shape[0] // LN_ROWS, rows, 0)


def _out_proj_kernel(a_ref, b_ref, w_ref, x_ref, g_ref, beta_ref, o_ref, ob_ref, *, nk, alpha):
    k = pl.program_id(1)
    half = nk // 2

    @pl.when(k == 0)
    def _():
        o_ref[...] = jnp.zeros_like(o_ref)

    def accumulate(lhs_ref):
        n = o_ref.shape[1]
        nb = min(n, 512)
        lhs = lhs_ref[...]
        for n0 in range(0, n, nb):
            o_ref[:, n0:n0 + nb] += jnp.dot(lhs, w_ref[:, n0:n0 + nb], preferred_element_type=F32)

    @pl.when(k < half)
    def _():
        accumulate(a_ref)

    @pl.when(k >= half)
    def _():
        accumulate(b_ref)

    @pl.when(k == nk - 1)
    def _():
        _deepnorm_epilogue(x_ref, o_ref, g_ref, beta_ref, alpha, (o_ref, ob_ref))


def _out_proj_ln(a, b, w, x, g, beta, *, bm, bk, alpha):
    m, ka = a.shape
    n = w.shape[1]
    half = ka // bk
    nk = 2 * half
    kern = functools.partial(_out_proj_kernel, nk=nk, alpha=alpha)
    return pl.pallas_call(
        kern,
        grid=(m // bm, nk),
        in_specs=[
            pl.BlockSpec((bm, bk), lambda i, k: (i, jnp.minimum(k, half - 1))),
            pl.BlockSpec((bm, bk), lambda i, k: (i, jnp.maximum(k - half, 0))),
            pl.BlockSpec((bk, n), lambda i, k: (k, 0)),
            pl.BlockSpec((bm, n), lambda i, k: (i, 0)),
            pl.BlockSpec((1, n), lambda i, k: (0, 0)),
            pl.BlockSpec((1, n), lambda i, k: (0, 0)),
        ],
        out_specs=[pl.BlockSpec((bm, n), lambda i, k: (i, 0)),
                   pl.BlockSpec((bm, n), lambda i, k: (i, 0))],
        out_shape=[jax.ShapeDtypeStruct((m, n), F32), jax.ShapeDtypeStruct((m, n), BF16)],
        compiler_params=pltpu.CompilerParams(
            dimension_semantics=("parallel", "arbitrary"), vmem_limit_bytes=VMEM_LIMIT),
        name="out_proj_ln",
    )(a, b, w, x, g.reshape(1, n), beta.reshape(1, n))


def _ffn_kernel(xb_ref, wg_ref, wu_ref, wd_ref, x_ref, g_ref, beta_ref, o_ref, *, nf, alpha):
    f = pl.program_id(1)

    @pl.when(f == 0)
    def _():
        o_ref[...] = jnp.zeros_like(o_ref)

    xb = xb_ref[...]
    gate = jnp.dot(xb, wg_ref[...], preferred_element_type=F32)
    up = jnp.dot(xb, wu_ref[...], preferred_element_type=F32)
    hid = (_silu(gate) * up).astype(BF16)
    dm = o_ref.shape[1]
    nb = min(dm, 512)
    for n0 in range(0, dm, nb):
        o_ref[:, n0:n0 + nb] += jnp.dot(hid, wd_ref[:, n0:n0 + nb], preferred_element_type=F32)

    @pl.when(f == nf - 1)
    def _():
        _deepnorm_epilogue(x_ref, o_ref, g_ref, beta_ref, alpha, (o_ref,))


def _ffn_ln(xb, x, wg, wu, wd, g, beta, *, bm, bf, alpha):
    m, dm = xb.shape
    ff = wg.shape[1]
    nf = ff // bf
    assert ff % bf == 0
    kern = functools.partial(_ffn_kernel, nf=nf, alpha=alpha)
    return pl.pallas_call(
        kern,
        grid=(m // bm, nf),
        in_specs=[
            pl.BlockSpec((bm, dm), lambda i, f: (i, 0)),
            pl.BlockSpec((dm, bf), lambda i, f: (0, f)),
            pl.BlockSpec((dm, bf), lambda i, f: (0, f)),
            pl.BlockSpec((bf, dm), lambda i, f: (f, 0)),
            pl.BlockSpec((bm, dm), lambda i, f: (i, 0)),
            pl.BlockSpec((1, dm), lambda i, f: (0, 0)),
            pl.BlockSpec((1, dm), lambda i, f: (0, 0)),
        ],
        out_specs=pl.BlockSpec((bm, dm), lambda i, f: (i, 0)),
        out_shape=jax.ShapeDtypeStruct((m, dm), F32),
        compiler_params=pltpu.CompilerParams(
            dimension_semantics=("parallel", "arbitrary"), vmem_limit_bytes=VMEM_LIMIT),
        name="ffn_ln",
    )(xb, wg, wu, wd, x, g.reshape(1, dm), beta.reshape(1, dm))


def _layer(x2, batch, seq, l, w_in, conv_w, a_log, dt_bias, gdn_norm_w, lq1, lk1, lq2, lk2,
           diff_norm_w, w_out, ln1_g, ln1_b, w_gate, w_up, w_down, ln2_g, ln2_b, alpha):
    m, dm = x2.shape
    n_gdn_heads = a_log.shape[0]
    gdn_w = n_gdn_heads * GDN_HEAD_DIM
    diff_w = w_out.shape[0] - gdn_w
    n_diff_heads = diff_w // DIFF_V_DIM
    lambda_init = 0.8 - 0.6 * math.exp(-0.3 * l)
    nc = seq // CHUNK

    bn = 768
    main = 4 * gdn_w
    ab = 2 * n_gdn_heads
    wide = -(-(main + ab) // bn) * bn
    w_t = w_in.T
    xb = x2.astype(BF16)

    proj_g = _matmul_wt32(xb, w_t, 0, wide, 1024, bn, F32, "proj_gdn")
    proj_d = _matmul_wt32(xb, w_t, main + ab, w_t.shape[0] - main - ab, 1024, bn, BF16, "proj_diff")

    gates = proj_g[:, main:main + ab].reshape(batch, seq, ab).transpose(0, 2, 1)
    gates = gates.reshape(batch, ab, nc, CHUNK)
    gates = jnp.pad(gates, ((0, 0), (0, 0), (0, 0), (0, LANES - CHUNK)))
    gdn_out = _gdn(proj_g, conv_w, gates[:, :n_gdn_heads], gates[:, n_gdn_heads:], a_log, dt_bias,
                   gdn_norm_w, batch=batch, seq=seq, n_heads=n_gdn_heads, heads=2)

    diff_out = _diff_attn(proj_d, lq1, lk1, lq2, lk2, diff_norm_w, batch=batch, seq=seq,
                          n_heads=n_diff_heads, tq=512, lambda_init=lambda_init)

    x1, x1b = _out_proj_ln(gdn_out, diff_out, w_out.astype(BF16), x2, ln1_g, ln1_b,
                           bm=512, bk=512, alpha=alpha)
    return _ffn_ln(x1b, x1, w_gate.astype(BF16), w_up.astype(BF16), w_down.astype(BF16),
                   ln2_g, ln2_b, bm=512, bf=256, alpha=alpha)


def kernel(x, w_in, conv_w, a_log, dt_bias, gdn_norm_w, lambda_q1, lambda_k1, lambda_q2, lambda_k2,
           diff_norm_w, w_out, ln1_g, ln1_b, w_gate, w_up, w_down, ln2_g, ln2_b):
    batch, seq, dm = x.shape
    depth = w_in.shape[0]
    alpha = (2 * depth) ** 0.25
    x2 = x.reshape(batch * seq, dm)
    for l in range(depth):
        x2 = _layer(x2, batch, seq, l, w_in[l], conv_w[l], a_log[l], dt_bias[l], gdn_norm_w[l],
                    lambda_q1[l], lambda_k1[l], lambda_q2[l], lambda_k2[l], diff_norm_w[l], w_out[l],
                    ln1_g[l], ln1_b[l], w_gate[l], w_up[l], w_down[l], ln2_g[l], ln2_b[l], alpha)
    return x2.reshape(batch, seq, dm)
```

```python
import functools
import math

import jax
import jax.numpy as jnp
from jax import lax
from jax.experimental import pallas as pl
from jax.experimental.pallas import tpu as pltpu

F32 = jnp.float32
BF16 = jnp.bfloat16

GDN_HEAD_DIM = 128
DIFF_HEAD_DIM = 128
DIFF_V_DIM = 256
CONV_K = 4
CHUNK = 64
PROLOGUE_ROWS = 128
LANES = 128
VMEM_LIMIT = 60 * 1024 * 1024


def _dot(a, b):
    return jnp.dot(a.astype(BF16), b.astype(BF16), preferred_element_type=F32)


def _dot_nt(a, b):
    return lax.dot_general(a.astype(BF16), b.astype(BF16), (((1,), (1,)), ((), ())),
                           preferred_element_type=F32)


def _dot_tn(a, b):
    return lax.dot_general(a.astype(BF16), b.astype(BF16), (((0,), (0,)), ((), ())),
                           preferred_element_type=F32)


def _silu(x):
    return x * jax.nn.sigmoid(x)


CAST_ROWS = 128


def _mm_wt32_kernel(x_ref, w_ref, o_ref, wb_ref, *, transposed_out):
    @pl.when(pl.program_id(1) == 0)
    def _():
        for r in range(0, w_ref.shape[0], CAST_ROWS):
            wb_ref[r:r + CAST_ROWS, :] = w_ref[r:r + CAST_ROWS, :].astype(BF16)

    if transposed_out:
        o_ref[...] = _dot_nt(wb_ref[...], x_ref[...]).astype(o_ref.dtype)
    else:
        o_ref[...] = _dot_nt(x_ref[...], wb_ref[...]).astype(o_ref.dtype)


def _matmul_wt32(x, w_t, row0, n_out, bm, bn, out_dtype, name, transposed_out=False):
    m, k = x.shape
    assert m % bm == 0 and n_out % bn == 0 and bn % CAST_ROWS == 0 and row0 % 8 == 0
    assert row0 + n_out <= w_t.shape[0] and k == w_t.shape[1]
    if transposed_out:
        out_spec = pl.BlockSpec((bn, bm), lambda j, i: (j, i))
        out_shape = jax.ShapeDtypeStruct((n_out, m), out_dtype)
    else:
        out_spec = pl.BlockSpec((bm, bn), lambda j, i: (i, j))
        out_shape = jax.ShapeDtypeStruct((m, n_out), out_dtype)
    return pl.pallas_call(
        functools.partial(_mm_wt32_kernel, transposed_out=transposed_out),
        grid=(n_out // bn, m // bm),
        in_specs=[pl.BlockSpec((bm, k), lambda j, i: (i, 0)),
                  pl.BlockSpec((pl.Element(bn), pl.Element(k)), lambda j, i: ((row0 // 8 + j * (bn // 8)) * 8, 0))],
        out_specs=out_spec,
        out_shape=out_shape,
        scratch_shapes=[pltpu.VMEM((bn, k), BF16)],
        compiler_params=pltpu.CompilerParams(
            dimension_semantics=("parallel", "arbitrary"), vmem_limit_bytes=VMEM_LIMIT),
        name=name,
    )(x, w_t)


def _gdn_kernel(q_ref, k_ref, v_ref, z_ref, cwq_ref, cwk_ref, cwv_ref, ga_ref, gb_ref,
                alog_ref, dtb_ref, nw_ref, o_ref,
                xpad, qs, ks, vs, q2s, k2s, o0s, rs, gcs, bts, egs, *, heads, seq, unroll1, unroll2):
    d = GDN_HEAD_DIM
    c_ = CHUNK
    nc = seq // c_
    row = lax.broadcasted_iota(jnp.int32, (c_, c_), 0)
    col = lax.broadcasted_iota(jnp.int32, (c_, c_), 1)
    tril = col <= row
    strict = col < row
    eye = col == row
    col2 = lax.broadcasted_iota(jnp.int32, (c_, 2 * c_), 1)
    right = col2 >= c_
    eye_r = jnp.where(col2 == lax.broadcasted_iota(jnp.int32, (c_, 2 * c_), 0) + c_, 1.0, 0.0)
    lane = lax.broadcasted_iota(jnp.int32, (nc, LANES), 1)

    rb = min(seq, PROLOGUE_ROWS)
    halo = CONV_K - 1

    def conv_silu(x_ref, cw_ref, dst, g, post):
        cols = slice(g * d, (g + 1) * d)
        w = cw_ref[:, cols]
        xpad[0:8, :] = jnp.zeros((8, d), F32)
        xpad[8:8 + rb, :] = x_ref[0:rb, cols]
        for r0 in range(0, seq, rb):
            acc = None
            for j in range(CONV_K):
                if r0 == 0:
                    xs = xpad[8 - halo + j:8 - halo + j + rb, :]
                else:
                    xs = x_ref[r0 - halo + j:r0 - halo + j + rb, cols]
                t = xs * w[j:j + 1, :]
                acc = t if acc is None else acc + t
            dst[g, r0:r0 + rb, :] = post(_silu(acc))

    l2n = lambda y: y * lax.rsqrt(jnp.sum(y * y, -1, keepdims=True) + 1e-6)

    for g in range(heads):
        conv_silu(q_ref, cwq_ref, qs, g, lambda y: l2n(y) * (d ** -0.5))
        conv_silu(k_ref, cwk_ref, ks, g, l2n)
        conv_silu(v_ref, cwv_ref, vs, g, lambda y: y)

        a = ga_ref[0, g] + dtb_ref[g]
        sp = jnp.maximum(a, 0.0) + jnp.log1p(jnp.exp(-jnp.abs(a)))
        gl = jnp.where(lane < c_, -jnp.exp(alog_ref[g]) * sp, 0.0)
        sh = 1
        while sh < c_:
            gl = gl + jnp.where(lane >= sh, pltpu.roll(gl, sh, 1), 0.0)
            sh *= 2
        gcs[g] = gl
        bts[g] = jax.nn.sigmoid(gb_ref[0, g])

    def phase1(cg, carry):
        chains = [(cg * unroll1 + j, g) for j in range(unroll1) for g in range(heads)]
        st = []
        for c, g in chains:
            r0 = pl.multiple_of(c * c_, c_)
            qc = qs[g, pl.ds(r0, c_), :]
            kc = ks[g, pl.ds(r0, c_), :]
            vc = vs[g, pl.ds(r0, c_), :]
            gcr = gcs[g, pl.ds(c, 1), :][:, :c_]
            btr = bts[g, pl.ds(c, 1), :][:, :c_]
            gcol = jnp.sum(jnp.where(eye, jnp.broadcast_to(gcr, (c_, c_)), 0.0), 1, keepdims=True)
            bcol = jnp.sum(jnp.where(eye, jnp.broadcast_to(btr, (c_, c_)), 0.0), 1, keepdims=True)
            glast = gcr[:, c_ - 1:c_]
            decay = jnp.where(tril, jnp.exp(jnp.where(tril, gcol - gcr, 0.0)), 0.0)
            kb = kc * bcol
            r = jnp.concatenate([vc * bcol, kb * jnp.exp(gcol)], axis=1)
            st.append(dict(c=c, g=g, r0=r0, qc=qc, kc=kc, kb=kb, gcol=gcol, glast=glast, decay=decay, r=r))
        for t in st:
            t["aq"] = _dot_nt(jnp.concatenate([t["kb"], t["qc"]], axis=0), t["kc"])
        for t in st:
            t["nl"] = jnp.where(strict, -(t["aq"][:c_] * t["decay"]), 0.0)
            t["intra"] = jnp.where(tril, t["aq"][c_:] * t["decay"], 0.0)
        n_levels = c_.bit_length() - 1
        for t in st:
            nt0 = jnp.concatenate([t["nl"], jnp.zeros((c_, c_), F32)], axis=1) + eye_r
            t["nt"] = _dot(t["nl"], nt0) + eye_r
        for lvl in range(1, n_levels):
            for t in st:
                nt = t["nt"]
                t["nt"] = _dot(nt[:, :c_], nt) + jnp.where(right, nt, 0.0)
        for t in st:
            t["r"] = _dot(t["nt"], jnp.concatenate([jnp.zeros_like(t["r"]), t["r"]], axis=0))
        for t in st:
            kd = t["kc"] * jnp.exp(t["glast"] - t["gcol"])
            t["kq"] = _dot_tn(kd, t["r"])
            t["io"] = _dot(t["intra"], t["r"])
        for t in st:
            c, g, r0 = t["c"], t["g"], t["r0"]
            rd = pl.multiple_of(c * d, d)
            q2s[g, pl.ds(rd, d), :] = t["kq"][:, :d]
            k2s[g, pl.ds(rd, d), :] = t["kq"][:, d:].astype(BF16)
            o0s[g, pl.ds(r0, c_), :] = t["io"][:, :d]
            rs[g, pl.ds(r0, c_), :] = (t["qc"] * jnp.exp(t["gcol"]) - t["io"][:, d:]).astype(BF16)
            egs[g, pl.ds(c, 1), :] = jnp.broadcast_to(jnp.exp(t["glast"]), (1, LANES))
        return carry

    lax.fori_loop(0, nc // unroll1, phase1, 0)

    def phase2(c, states):
        r0 = pl.multiple_of(c * c_, c_)
        rd = pl.multiple_of(c * d, d)
        sbs = [states[g].astype(BF16) for g in range(heads)]
        ks2 = [_dot(k2s[g, pl.ds(rd, d), :], sbs[g]) for g in range(heads)]
        ros = [_dot(rs[g, pl.ds(r0, c_), :], sbs[g]) for g in range(heads)]
        new_states = [states[g] * egs[g, pl.ds(c, 1), :] + q2s[g, pl.ds(rd, d), :] - ks2[g]
                      for g in range(heads)]
        for g in range(heads):
            o0s[g, pl.ds(r0, c_), :] += ros[g]
        return tuple(new_states)

    lax.fori_loop(0, nc, phase2, tuple(jnp.zeros((d, d), F32) for _ in range(heads)), unroll=unroll2)

    nw = nw_ref[...]
    for g in range(heads):
        cols = slice(g * d, (g + 1) * d)
        for r0 in range(0, seq, rb):
            o = o0s[g, r0:r0 + rb, :]
            on = o * lax.rsqrt(jnp.mean(o * o, -1, keepdims=True) + 1e-6) * nw
            o_ref[r0:r0 + rb, cols] = (on * _silu(z_ref[r0:r0 + rb, cols])).astype(o_ref.dtype)


def _gdn(proj, conv_w, ga_rows, gb_rows, a_log, dt_bias, norm_w, *, batch, seq, n_heads, heads):
    d = GDN_HEAD_DIM
    gw = heads * d
    hb = n_heads // heads
    nc = seq // CHUNK
    m = batch * seq
    kern = functools.partial(_gdn_kernel, heads=heads, seq=seq, unroll1=math.gcd(nc, 8), unroll2=2)
    col = lambda off: (lambda b, h: (b, off * hb + h))
    cw = lambda off: (lambda b, h: (0, off * hb + h))
    return pl.pallas_call(
        kern,
        grid=(batch, hb),
        in_specs=[
            pl.BlockSpec((seq, gw), col(0)), pl.BlockSpec((seq, gw), col(1)),
            pl.BlockSpec((seq, gw), col(2)), pl.BlockSpec((seq, gw), col(3)),
            pl.BlockSpec((CONV_K, gw), cw(0)), pl.BlockSpec((CONV_K, gw), cw(1)),
            pl.BlockSpec((CONV_K, gw), cw(2)),
            pl.BlockSpec((1, heads, nc, LANES), lambda b, h: (b, h, 0, 0)),
            pl.BlockSpec((1, heads, nc, LANES), lambda b, h: (b, h, 0, 0)),
            pl.BlockSpec((heads, 1, 1), lambda b, h: (h, 0, 0)),
            pl.BlockSpec((heads, 1, 1), lambda b, h: (h, 0, 0)),
            pl.BlockSpec((1, d), lambda b, h: (0, 0)),
        ],
        out_specs=pl.BlockSpec((seq, gw), lambda b, h: (b, h)),
        out_shape=jax.ShapeDtypeStruct((m, n_heads * d), BF16),
        scratch_shapes=[
            pltpu.VMEM((PROLOGUE_ROWS + 8, d), F32),
            pltpu.VMEM((heads, seq, d), F32),
            pltpu.VMEM((heads, seq, d), F32),
            pltpu.VMEM((heads, seq, d), F32),
            pltpu.VMEM((heads, nc * d, d), F32),
            pltpu.VMEM((heads, nc * d, d), BF16),
            pltpu.VMEM((heads, seq, d), F32),
            pltpu.VMEM((heads, seq, d), BF16),
            pltpu.VMEM((heads, nc, LANES), F32),
            pltpu.VMEM((heads, nc, LANES), F32),
            pltpu.VMEM((heads, nc, LANES), F32),
        ],
        compiler_params=pltpu.CompilerParams(
            dimension_semantics=("parallel", "parallel"), vmem_limit_bytes=VMEM_LIMIT),
        name="gdn",
    )(proj, proj, proj, proj, conv_w, conv_w, conv_w, ga_rows, gb_rows,
      a_log.reshape(n_heads, 1, 1), dt_bias.reshape(n_heads, 1, 1), norm_w.reshape(1, d))


def _diff_attn_kernel(q_ref, k_ref, vt_ref, lq1_ref, lk1_ref, lq2_ref, lk2_ref, nw_ref, o_ref,
                      acc1, acc2, *, tq, seq, lambda_init):
    dh = DIFF_HEAD_DIM
    c2 = (dh ** -0.5) * math.log2(math.e)
    neg = float(jnp.finfo(jnp.float32).min)
    krow = lax.broadcasted_iota(jnp.int32, (tq, tq), 0)
    qcol = lax.broadcasted_iota(jnp.int32, (tq, tq), 1)
    keep = krow <= qcol
    lam = (jnp.exp(jnp.sum(lq1_ref[...] * lk1_ref[...], -1, keepdims=True))
           - jnp.exp(jnp.sum(lq2_ref[...] * lk2_ref[...], -1, keepdims=True)) + lambda_init)
    nw = nw_ref[...] * (1.0 - lambda_init)

    for i in range(seq // tq):
        q1 = q_ref[i * tq:(i + 1) * tq, :dh]
        q2 = q_ref[i * tq:(i + 1) * tq, dh:]
        m1 = m2 = jnp.full((1, tq), neg, F32)
        l1 = l2 = jnp.zeros((1, tq), F32)
        for j in range(i + 1):
            kj = k_ref[j * tq:(j + 1) * tq, :]
            vt = vt_ref[:, j * tq:(j + 1) * tq]
            s1 = _dot_nt(kj[:, :dh], q1)
            s2 = _dot_nt(kj[:, dh:], q2)
            if j == i:
                s1 = jnp.where(keep, s1, neg)
                s2 = jnp.where(keep, s2, neg)
            m1n = jnp.maximum(m1, jnp.max(s1, 0, keepdims=True))
            m2n = jnp.maximum(m2, jnp.max(s2, 0, keepdims=True))
            a1 = jnp.exp2((m1 - m1n) * c2)
            a2 = jnp.exp2((m2 - m2n) * c2)
            p1 = jnp.exp2((s1 - m1n) * c2)
            p2 = jnp.exp2((s2 - m2n) * c2)
            l1 = a1 * l1 + jnp.sum(p1, 0, keepdims=True)
            l2 = a2 * l2 + jnp.sum(p2, 0, keepdims=True)
            pv1 = _dot(vt, p1)
            pv2 = _dot(vt, p2)
            if j == 0:
                acc1[...] = pv1
                acc2[...] = pv2
            else:
                acc1[...] = acc1[...] * a1 + pv1
                acc2[...] = acc2[...] * a2 + pv2
            m1, m2 = m1n, m2n
        o = acc1[...] / l1 - lam * (acc2[...] / l2)
        on = o * lax.rsqrt(jnp.mean(o * o, 0, keepdims=True) + 1e-6) * nw
        o_ref[i * tq:(i + 1) * tq, :] = on.T.astype(o_ref.dtype)


def _diff_attn(qk, vt, lq1, lk1, lq2, lk2, norm_w, *, batch, seq, n_heads, tq, lambda_init):
    dv = DIFF_V_DIM
    m = batch * seq
    kern = functools.partial(_diff_attn_kernel, tq=tq, seq=seq, lambda_init=lambda_init)
    vec = lambda n: pl.BlockSpec((1, n), lambda b, h: (0, 0))
    return pl.pallas_call(
        kern,
        grid=(batch, n_heads),
        in_specs=[
            pl.BlockSpec((seq, dv), lambda b, h: (b, h)),
            pl.BlockSpec((seq, dv), lambda b, h: (b, n_heads + h)),
            pl.BlockSpec((dv, seq), lambda b, h: (h, b)),
            vec(DIFF_HEAD_DIM), vec(DIFF_HEAD_DIM), vec(DIFF_HEAD_DIM), vec(DIFF_HEAD_DIM),
            pl.BlockSpec((dv, 1), lambda b, h: (0, 0)),
        ],
        out_specs=pl.BlockSpec((seq, dv), lambda b, h: (b, h)),
        out_shape=jax.ShapeDtypeStruct((m, n_heads * dv), BF16),
        scratch_shapes=[pltpu.VMEM((dv, tq), F32), pltpu.VMEM((dv, tq), F32)],
        compiler_params=pltpu.CompilerParams(
            dimension_semantics=("parallel", "parallel"), vmem_limit_bytes=VMEM_LIMIT),
        name="diff_attn",
    )(qk, qk, vt, lq1.reshape(1, -1), lk1.reshape(1, -1), lq2.reshape(1, -1),
      lk2.reshape(1, -1), norm_w.reshape(-1, 1))


def _layer_norm(y, g, b):
    mu = jnp.mean(y, -1, keepdims=True)
    yc = y - mu
    var = jnp.mean(yc * yc, -1, keepdims=True)
    return yc * lax.rsqrt(var + 1e-5) * g + b


LN_ROWS = 64


def _deepnorm_epilogue(x_ref, acc_ref, g_ref, beta_ref, alpha, out_refs):
    g = g_ref[...]
    beta = beta_ref[...]

    def rows(i, carry):
        r0 = pl.multiple_of(i * LN_ROWS, LN_ROWS)
        sl = pl.ds(r0, LN_ROWS)
        y = _layer_norm(alpha * x_ref[sl, :] + acc_ref[sl, :], g, beta)
        for o in out_refs:
            o[sl, :] = y.astype(o.dtype)
        return carry

    lax.fori_loop(0, x_ref.shape[0] // LN_ROWS, rows, 0)


def _out_proj_kernel(a_ref, b_ref, w_ref, x_ref, g_ref, beta_ref, o_ref, ob_ref, *, nk, alpha):
    k = pl.program_id(1)
    half = nk // 2

    @pl.when(k == 0)
    def _():
        o_ref[...] = jnp.zeros_like(o_ref)

    def accumulate(lhs_ref):
        n = o_ref.shape[1]
        nb = min(n, 512)
        lhs = lhs_ref[...]
        for n0 in range(0, n, nb):
            o_ref[:, n0:n0 + nb] += jnp.dot(lhs, w_ref[:, n0:n0 + nb], preferred_element_type=F32)

    @pl.when(k < half)
    def _():
        accumulate(a_ref)

    @pl.when(k >= half)
    def _():
        accumulate(b_ref)

    @pl.when(k == nk - 1)
    def _():
        _deepnorm_epilogue(x_ref, o_ref, g_ref, beta_ref, alpha, (o_ref, ob_ref))


def _out_proj_ln(a, b, w, x, g, beta, *, bm, bk, alpha):
    m, ka = a.shape
    n = w.shape[1]
    half = ka // bk
    nk = 2 * half
    kern = functools.partial(_out_proj_kernel, nk=nk, alpha=alpha)
    return pl.pallas_call(
        kern,
        grid=(m // bm, nk),
        in_specs=[
            pl.BlockSpec((bm, bk), lambda i, k: (i, jnp.minimum(k, half - 1))),
            pl.BlockSpec((bm, bk), lambda i, k: (i, jnp.maximum(k - half, 0))),
            pl.BlockSpec((bk, n), lambda i, k: (k, 0)),
            pl.BlockSpec((bm, n), lambda i, k: (i, 0)),
            pl.BlockSpec((1, n), lambda i, k: (0, 0)),
            pl.BlockSpec((1, n), lambda i, k: (0, 0)),
        ],
        out_specs=[pl.BlockSpec((bm, n), lambda i, k: (i, 0)),
                   pl.BlockSpec((bm, n), lambda i, k: (i, 0))],
        out_shape=[jax.ShapeDtypeStruct((m, n), F32), jax.ShapeDtypeStruct((m, n), BF16)],
        compiler_params=pltpu.CompilerParams(
            dimension_semantics=("parallel", "arbitrary"), vmem_limit_bytes=VMEM_LIMIT),
        name="out_proj_ln",
    )(a, b, w, x, g.reshape(1, n), beta.reshape(1, n))


def _ffn_kernel(xb_ref, wg_ref, wu_ref, wd_ref, x_ref, g_ref, beta_ref, o_ref, *, nf, alpha):
    f = pl.program_id(1)

    @pl.when(f == 0)
    def _():
        o_ref[...] = jnp.zeros_like(o_ref)

    xb = xb_ref[...]
    gate = jnp.dot(xb, wg_ref[...], preferred_element_type=F32)
    up = jnp.dot(xb, wu_ref[...], preferred_element_type=F32)
    hid = (_silu(gate) * up).astype(BF16)
    dm = o_ref.shape[1]
    nb = min(dm, 512)
    for n0 in range(0, dm, nb):
        o_ref[:, n0:n0 + nb] += jnp.dot(hid, wd_ref[:, n0:n0 + nb], preferred_element_type=F32)

    @pl.when(f == nf - 1)
    def _():
        _deepnorm_epilogue(x_ref, o_ref, g_ref, beta_ref, alpha, (o_ref,))


def _ffn_ln(xb, x, wg, wu, wd, g, beta, *, bm, bf, alpha):
    m, dm = xb.shape
    ff = wg.shape[1]
    nf = ff // bf
    assert ff % bf == 0
    kern = functools.partial(_ffn_kernel, nf=nf, alpha=alpha)
    return pl.pallas_call(
        kern,
        grid=(m // bm, nf),
        in_specs=[
            pl.BlockSpec((bm, dm), lambda i, f: (i, 0)),
            pl.BlockSpec((dm, bf), lambda i, f: (0, f)),
            pl.BlockSpec((dm, bf), lambda i, f: (0, f)),
            pl.BlockSpec((bf, dm), lambda i, f: (f, 0)),
            pl.BlockSpec((bm, dm), lambda i, f: (i, 0)),
            pl.BlockSpec((1, dm), lambda i, f: (0, 0)),
            pl.BlockSpec((1, dm), lambda i, f: (0, 0)),
        ],
        out_specs=pl.BlockSpec((bm, dm), lambda i, f: (i, 0)),
        out_shape=jax.ShapeDtypeStruct((m, dm), F32),
        compiler_params=pltpu.CompilerParams(
            dimension_semantics=("parallel", "arbitrary"), vmem_limit_bytes=VMEM_LIMIT),
        name="ffn_ln",
    )(xb, wg, wu, wd, x, g.reshape(1, dm), beta.reshape(1, dm))


def _layer(x2, batch, seq, l, w_in, conv_w, a_log, dt_bias, gdn_norm_w, lq1, lk1, lq2, lk2,
           diff_norm_w, w_out, ln1_g, ln1_b, w_gate, w_up, w_down, ln2_g, ln2_b, alpha):
    m, dm = x2.shape
    n_gdn_heads = a_log.shape[0]
    gdn_w = n_gdn_heads * GDN_HEAD_DIM
    diff_w = w_out.shape[0] - gdn_w
    n_diff_heads = diff_w // DIFF_V_DIM
    lambda_init = 0.8 - 0.6 * math.exp(-0.3 * l)
    nc = seq // CHUNK

    bn = 768
    main = 4 * gdn_w
    ab = 2 * n_gdn_heads
    wide = -(-(main + ab) // bn) * bn
    w_t = w_in.T
    xb = x2.astype(BF16)

    proj_g = _matmul_wt32(xb, w_t, 0, wide, 1024, bn, F32, "proj_gdn")
    qk_w = 2 * n_diff_heads * 2 * DIFF_HEAD_DIM
    proj_qk = _matmul_wt32(xb, w_t, main + ab, qk_w, 1024, 512, BF16, "proj_qk")
    proj_vt = _matmul_wt32(xb, w_t, main + ab + qk_w, diff_w, 1024, 512, BF16, "proj_vt",
                           transposed_out=True)

    gates = proj_g[:, main:main + ab].reshape(batch, seq, ab).transpose(0, 2, 1)
    gates = gates.reshape(batch, ab, nc, CHUNK)
    gates = jnp.pad(gates, ((0, 0), (0, 0), (0, 0), (0, LANES - CHUNK)))
    gdn_out = _gdn(proj_g, conv_w, gates[:, :n_gdn_heads], gates[:, n_gdn_heads:], a_log, dt_bias,
                   gdn_norm_w, batch=batch, seq=seq, n_heads=n_gdn_heads, heads=2)

    diff_out = _diff_attn(proj_qk, proj_vt, lq1, lk1, lq2, lk2, diff_norm_w, batch=batch, seq=seq,
                          n_heads=n_diff_heads, tq=512, lambda_init=lambda_init)

    x1, x1b = _out_proj_ln(gdn_out, diff_out, w_out.astype(BF16), x2, ln1_g, ln1_b,
                           bm=512, bk=512, alpha=alpha)
    return _ffn_ln(x1b, x1, w_gate.astype(BF16), w_up.astype(BF16), w_down.astype(BF16),
                   ln2_g, ln2_b, bm=512, bf=256, alpha=alpha)


def kernel(x, w_in, conv_w, a_log, dt_bias, gdn_norm_w, lambda_q1, lambda_k1, lambda_q2, lambda_k2,
           diff_norm_w, w_out, ln1_g, ln1_b, w_gate, w_up, w_down, ln2_g, ln2_b):
    batch, seq, dm = x.shape
    depth = w_in.shape[0]
    alpha = (2 * depth) ** 0.25
    x2 = x.reshape(batch * seq, dm)
    for l in range(depth):
        x2 = _layer(x2, batch, seq, l, w_in[l], conv_w[l], a_log[l], dt_bias[l], gdn_norm_w[l],
                    lambda_q1[l], lambda_k1[l], lambda_q2[l], lambda_k2[l], diff_norm_w[l], w_out[l],
                    ln1_g[l], ln1_b[l], w_gate[l], w_up[l], w_down[l], ln2_g[l], ln2_b[l], alpha)
    return x2.reshape(batch, seq, dm)
```

```python
import functools
import math

import jax
import jax.numpy as jnp
from jax import lax
from jax.experimental import pallas as pl
from jax.experimental.pallas import tpu as pltpu

F32 = jnp.float32
BF16 = jnp.bfloat16

GDN_HEAD_DIM = 128
DIFF_HEAD_DIM = 128
DIFF_V_DIM = 256
CONV_K = 4
CHUNK = 64
PROLOGUE_ROWS = 128
LANES = 128
VMEM_LIMIT = 60 * 1024 * 1024


def _dot(a, b):
    return jnp.dot(a.astype(BF16), b.astype(BF16), preferred_element_type=F32)


def _dot_nt(a, b):
    return lax.dot_general(a.astype(BF16), b.astype(BF16), (((1,), (1,)), ((), ())),
                           preferred_element_type=F32)


def _dot_tn(a, b):
    return lax.dot_general(a.astype(BF16), b.astype(BF16), (((0,), (0,)), ((), ())),
                           preferred_element_type=F32)


def _silu(x):
    return x * jax.nn.sigmoid(x)


CAST_ROWS = 128


SIDE_COLS = 1024


def _cast_block(src_ref, dst_ref, col0=0, col1=None):
    col1 = src_ref.shape[1] if col1 is None else col1
    for c0 in range(col0, col1, SIDE_COLS):
        c1 = min(c0 + SIDE_COLS, col1)
        dst_ref[:, c0:c1] = src_ref[:, c0:c1].astype(BF16)


def _mm_wt32_kernel(x_ref, w_ref, *rest, transposed_out, has_side):
    if has_side:
        side_ref, o_ref, side_out_ref, wb_ref = rest
        _cast_block(side_ref, side_out_ref)
    else:
        o_ref, wb_ref = rest

    @pl.when(pl.program_id(1) == 0)
    def _():
        for r in range(0, w_ref.shape[0], CAST_ROWS):
            wb_ref[r:r + CAST_ROWS, :] = w_ref[r:r + CAST_ROWS, :].astype(BF16)

    if transposed_out:
        o_ref[...] = _dot_nt(wb_ref[...], x_ref[...]).astype(o_ref.dtype)
    else:
        o_ref[...] = _dot_nt(x_ref[...], wb_ref[...]).astype(o_ref.dtype)


def _matmul_wt32(x, w_t, row0, n_out, bm, bn, out_dtype, name, transposed_out=False, side=None):
    m, k = x.shape
    assert m % bm == 0 and n_out % bn == 0 and bn % CAST_ROWS == 0 and row0 % 8 == 0
    assert row0 + n_out <= w_t.shape[0] and k == w_t.shape[1]
    nj, ni = n_out // bn, m // bm
    if transposed_out:
        out_specs = [pl.BlockSpec((bn, bm), lambda j, i: (j, i))]
        out_shape = [jax.ShapeDtypeStruct((n_out, m), out_dtype)]
    else:
        out_specs = [pl.BlockSpec((bm, bn), lambda j, i: (i, j))]
        out_shape = [jax.ShapeDtypeStruct((m, n_out), out_dtype)]
    in_specs = [pl.BlockSpec((bm, k), lambda j, i: (i, 0)),
                pl.BlockSpec((pl.Element(bn), pl.Element(k)), lambda j, i: ((row0 // 8 + j * (bn // 8)) * 8, 0))]
    args = [x, w_t]
    if side is not None:
        a, rows = side
        n_slabs = a.shape[0] // rows
        assert a.shape[0] % rows == 0 and rows % 16 == 0 and n_slabs <= nj * ni
        slab = lambda j, i: (jnp.minimum(j * ni + i, n_slabs - 1), 0)
        in_specs.append(pl.BlockSpec((rows, a.shape[1]), slab))
        out_specs.append(pl.BlockSpec((rows, a.shape[1]), slab))
        out_shape.append(jax.ShapeDtypeStruct(a.shape, BF16))
        args.append(a)
    outs = pl.pallas_call(
        functools.partial(_mm_wt32_kernel, transposed_out=transposed_out, has_side=side is not None),
        grid=(nj, ni),
        in_specs=in_specs,
        out_specs=out_specs,
        out_shape=out_shape,
        scratch_shapes=[pltpu.VMEM((bn, k), BF16)],
        compiler_params=pltpu.CompilerParams(
            dimension_semantics=("arbitrary", "arbitrary"), vmem_limit_bytes=VMEM_LIMIT),
        name=name,
    )(*args)
    return outs if side is not None else outs[0]


def _gdn_kernel(q_ref, k_ref, v_ref, z_ref, cwq_ref, cwk_ref, cwv_ref, ga_ref, gb_ref,
                alog_ref, dtb_ref, nw_ref, o_ref,
                xpad, qs, ks, vs, q2s, k2s, o0s, rs, gcs, bts, egs, *, heads, seq, unroll1, unroll2):
    d = GDN_HEAD_DIM
    c_ = CHUNK
    nc = seq // c_
    row = lax.broadcasted_iota(jnp.int32, (c_, c_), 0)
    col = lax.broadcasted_iota(jnp.int32, (c_, c_), 1)
    tril = col <= row
    strict = col < row
    eye = col == row
    col2 = lax.broadcasted_iota(jnp.int32, (c_, 2 * c_), 1)
    right = col2 >= c_
    eye_r = jnp.where(col2 == lax.broadcasted_iota(jnp.int32, (c_, 2 * c_), 0) + c_, 1.0, 0.0)
    lane = lax.broadcasted_iota(jnp.int32, (nc, LANES), 1)

    rb = min(seq, PROLOGUE_ROWS)
    halo = CONV_K - 1

    def conv_silu(x_ref, cw_ref, dst, g, post):
        cols = slice(g * d, (g + 1) * d)
        w = cw_ref[:, cols]
        xpad[0:8, :] = jnp.zeros((8, d), F32)
        xpad[8:8 + rb, :] = x_ref[0:rb, cols]
        for r0 in range(0, seq, rb):
            acc = None
            for j in range(CONV_K):
                if r0 == 0:
                    xs = xpad[8 - halo + j:8 - halo + j + rb, :]
                else:
                    xs = x_ref[r0 - halo + j:r0 - halo + j + rb, cols]
                t = xs * w[j:j + 1, :]
                acc = t if acc is None else acc + t
            dst[g, r0:r0 + rb, :] = post(_silu(acc))

    l2n = lambda y: y * lax.rsqrt(jnp.sum(y * y, -1, keepdims=True) + 1e-6)

    for g in range(heads):
        conv_silu(q_ref, cwq_ref, qs, g, lambda y: l2n(y) * (d ** -0.5))
        conv_silu(k_ref, cwk_ref, ks, g, l2n)
        conv_silu(v_ref, cwv_ref, vs, g, lambda y: y)

        a = ga_ref[0, g] + dtb_ref[g]
        sp = jnp.maximum(a, 0.0) + jnp.log1p(jnp.exp(-jnp.abs(a)))
        gl = jnp.where(lane < c_, -jnp.exp(alog_ref[g]) * sp, 0.0)
        sh = 1
        while sh < c_:
            gl = gl + jnp.where(lane >= sh, pltpu.roll(gl, sh, 1), 0.0)
            sh *= 2
        gcs[g] = gl
        bts[g] = jax.nn.sigmoid(gb_ref[0, g])

    def phase1(cg, carry):
        chains = [(cg * unroll1 + j, g) for j in range(unroll1) for g in range(heads)]
        st = []
        for c, g in chains:
            r0 = pl.multiple_of(c * c_, c_)
            qc = qs[g, pl.ds(r0, c_), :]
            kc = ks[g, pl.ds(r0, c_), :]
            vc = vs[g, pl.ds(r0, c_), :]
            gcr = gcs[g, pl.ds(c, 1), :][:, :c_]
            btr = bts[g, pl.ds(c, 1), :][:, :c_]
            gcol = jnp.sum(jnp.where(eye, jnp.broadcast_to(gcr, (c_, c_)), 0.0), 1, keepdims=True)
            bcol = jnp.sum(jnp.where(eye, jnp.broadcast_to(btr, (c_, c_)), 0.0), 1, keepdims=True)
            glast = gcr[:, c_ - 1:c_]
            decay = jnp.where(tril, jnp.exp(jnp.where(tril, gcol - gcr, 0.0)), 0.0)
            kb = kc * bcol
            r = jnp.concatenate([vc * bcol, kb * jnp.exp(gcol)], axis=1)
            st.append(dict(c=c, g=g, r0=r0, qc=qc, kc=kc, kb=kb, gcol=gcol, glast=glast, decay=decay, r=r))
        for t in st:
            t["aq"] = _dot_nt(jnp.concatenate([t["kb"], t["qc"]], axis=0), t["kc"])
        for t in st:
            t["nl"] = jnp.where(strict, -(t["aq"][:c_] * t["decay"]), 0.0)
            t["intra"] = jnp.where(tril, t["aq"][c_:] * t["decay"], 0.0)
        n_levels = c_.bit_length() - 1
        for t in st:
            nt0 = jnp.concatenate([t["nl"], jnp.zeros((c_, c_), F32)], axis=1) + eye_r
            t["nt"] = _dot(t["nl"], nt0) + eye_r
        for lvl in range(1, n_levels):
            for t in st:
                nt = t["nt"]
                t["nt"] = _dot(nt[:, :c_], nt) + jnp.where(right, nt, 0.0)
        for t in st:
            t["r"] = _dot(t["nt"], jnp.concatenate([jnp.zeros_like(t["r"]), t["r"]], axis=0))
        for t in st:
            kd = t["kc"] * jnp.exp(t["glast"] - t["gcol"])
            t["kq"] = _dot_tn(kd, t["r"])
            t["io"] = _dot(t["intra"], t["r"])
        for t in st:
            c, g, r0 = t["c"], t["g"], t["r0"]
            rd = pl.multiple_of(c * d, d)
            q2s[g, pl.ds(rd, d), :] = t["kq"][:, :d]
            k2s[g, pl.ds(rd, d), :] = t["kq"][:, d:].astype(BF16)
            o0s[g, pl.ds(r0, c_), :] = t["io"][:, :d]
            rs[g, pl.ds(r0, c_), :] = (t["qc"] * jnp.exp(t["gcol"]) - t["io"][:, d:]).astype(BF16)
            egs[g, pl.ds(c, 1), :] = jnp.broadcast_to(jnp.exp(t["glast"]), (1, LANES))
        return carry

    lax.fori_loop(0, nc // unroll1, phase1, 0)

    def phase2(c, states):
        r0 = pl.multiple_of(c * c_, c_)
        rd = pl.multiple_of(c * d, d)
        sbs = [states[g].astype(BF16) for g in range(heads)]
        ks2 = [_dot(k2s[g, pl.ds(rd, d), :], sbs[g]) for g in range(heads)]
        ros = [_dot(rs[g, pl.ds(r0, c_), :], sbs[g]) for g in range(heads)]
        new_states = [states[g] * egs[g, pl.ds(c, 1), :] + q2s[g, pl.ds(rd, d), :] - ks2[g]
                      for g in range(heads)]
        for g in range(heads):
            o0s[g, pl.ds(r0, c_), :] += ros[g]
        return tuple(new_states)

    lax.fori_loop(0, nc, phase2, tuple(jnp.zeros((d, d), F32) for _ in range(heads)), unroll=unroll2)

    nw = nw_ref[...]
    for g in range(heads):
        cols = slice(g * d, (g + 1) * d)
        for r0 in range(0, seq, rb):
            o = o0s[g, r0:r0 + rb, :]
            on = o * lax.rsqrt(jnp.mean(o * o, -1, keepdims=True) + 1e-6) * nw
            o_ref[r0:r0 + rb, cols] = (on * _silu(z_ref[r0:r0 + rb, cols])).astype(o_ref.dtype)


def _gdn(proj, conv_w, ga_rows, gb_rows, a_log, dt_bias, norm_w, *, batch, seq, n_heads, heads):
    d = GDN_HEAD_DIM
    gw = heads * d
    hb = n_heads // heads
    nc = seq // CHUNK
    m = batch * seq
    kern = functools.partial(_gdn_kernel, heads=heads, seq=seq, unroll1=math.gcd(nc, 8), unroll2=2)
    col = lambda off: (lambda b, h: (b, off * hb + h))
    cw = lambda off: (lambda b, h: (0, off * hb + h))
    return pl.pallas_call(
        kern,
        grid=(batch, hb),
        in_specs=[
            pl.BlockSpec((seq, gw), col(0)), pl.BlockSpec((seq, gw), col(1)),
            pl.BlockSpec((seq, gw), col(2)), pl.BlockSpec((seq, gw), col(3)),
            pl.BlockSpec((CONV_K, gw), cw(0)), pl.BlockSpec((CONV_K, gw), cw(1)),
            pl.BlockSpec((CONV_K, gw), cw(2)),
            pl.BlockSpec((1, heads, nc, LANES), lambda b, h: (b, h, 0, 0)),
            pl.BlockSpec((1, heads, nc, LANES), lambda b, h: (b, h, 0, 0)),
            pl.BlockSpec((heads, 1, 1), lambda b, h: (h, 0, 0)),
            pl.BlockSpec((heads, 1, 1), lambda b, h: (h, 0, 0)),
            pl.BlockSpec((1, d), lambda b, h: (0, 0)),
        ],
        out_specs=pl.BlockSpec((seq, gw), lambda b, h: (b, h)),
        out_shape=jax.ShapeDtypeStruct((m, n_heads * d), BF16),
        scratch_shapes=[
            pltpu.VMEM((PROLOGUE_ROWS + 8, d), F32),
            pltpu.VMEM((heads, seq, d), F32),
            pltpu.VMEM((heads, seq, d), F32),
            pltpu.VMEM((heads, seq, d), F32),
            pltpu.VMEM((heads, nc * d, d), F32),
            pltpu.VMEM((heads, nc * d, d), BF16),
            pltpu.VMEM((heads, seq, d), F32),
            pltpu.VMEM((heads, seq, d), BF16),
            pltpu.VMEM((heads, nc, LANES), F32),
            pltpu.VMEM((heads, nc, LANES), F32),
            pltpu.VMEM((heads, nc, LANES), F32),
        ],
        compiler_params=pltpu.CompilerParams(
            dimension_semantics=("parallel", "parallel"), vmem_limit_bytes=VMEM_LIMIT),
        name="gdn",
    )(proj, proj, proj, proj, conv_w, conv_w, conv_w, ga_rows, gb_rows,
      a_log.reshape(n_heads, 1, 1), dt_bias.reshape(n_heads, 1, 1), norm_w.reshape(1, d))


def _diff_attn_kernel(q_ref, k_ref, vt_ref, lq1_ref, lk1_ref, lq2_ref, lk2_ref, nw_ref, sa_ref, sb_ref,
                      o_ref, sa_out_ref, sb_out_ref, acc1, acc2, *, tq, seq, lambda_init):
    dh = DIFF_HEAD_DIM
    c2 = (dh ** -0.5) * math.log2(math.e)
    neg = float(jnp.finfo(jnp.float32).min)
    krow = lax.broadcasted_iota(jnp.int32, (tq, tq), 0)
    qcol = lax.broadcasted_iota(jnp.int32, (tq, tq), 1)
    keep = krow <= qcol
    lam = (jnp.exp(jnp.sum(lq1_ref[...] * lk1_ref[...], -1, keepdims=True))
           - jnp.exp(jnp.sum(lq2_ref[...] * lk2_ref[...], -1, keepdims=True)) + lambda_init)
    nw = nw_ref[...] * (1.0 - lambda_init)

    nq = seq // tq
    side_cols = sa_ref.shape[1]
    piece = -(-side_cols // (nq * LANES)) * LANES
    for i in range(nq):
        for src, dst in ((sa_ref, sa_out_ref), (sb_ref, sb_out_ref)):
            if i * piece < side_cols:
                _cast_block(src, dst, i * piece, min((i + 1) * piece, side_cols))
        q1 = q_ref[i * tq:(i + 1) * tq, :dh]
        q2 = q_ref[i * tq:(i + 1) * tq, dh:]
        m1 = m2 = jnp.full((1, tq), neg, F32)
        l1 = l2 = jnp.zeros((1, tq), F32)
        for j in range(i + 1):
            kj = k_ref[j * tq:(j + 1) * tq, :]
            vt = vt_ref[:, j * tq:(j + 1) * tq]
            s1 = _dot_nt(kj[:, :dh], q1)
            s2 = _dot_nt(kj[:, dh:], q2)
            if j == i:
                s1 = jnp.where(keep, s1, neg)
                s2 = jnp.where(keep, s2, neg)
            m1n = jnp.maximum(m1, jnp.max(s1, 0, keepdims=True))
            m2n = jnp.maximum(m2, jnp.max(s2, 0, keepdims=True))
            a1 = jnp.exp2((m1 - m1n) * c2)
            a2 = jnp.exp2((m2 - m2n) * c2)
            p1 = jnp.exp2((s1 - m1n) * c2)
            p2 = jnp.exp2((s2 - m2n) * c2)
            l1 = a1 * l1 + jnp.sum(p1, 0, keepdims=True)
            l2 = a2 * l2 + jnp.sum(p2, 0, keepdims=True)
            pv1 = _dot(vt, p1)
            pv2 = _dot(vt, p2)
            if j == 0:
                acc1[...] = pv1
                acc2[...] = pv2
            else:
                acc1[...] = acc1[...] * a1 + pv1
                acc2[...] = acc2[...] * a2 + pv2
            m1, m2 = m1n, m2n
        o = acc1[...] / l1 - lam * (acc2[...] / l2)
        on = o * lax.rsqrt(jnp.mean(o * o, 0, keepdims=True) + 1e-6) * nw
        o_ref[i * tq:(i + 1) * tq, :] = on.T.astype(o_ref.dtype)


def _diff_attn(qk, vt, lq1, lk1, lq2, lk2, norm_w, side_a, side_b, *, batch, seq, n_heads, tq, lambda_init):
    dv = DIFF_V_DIM
    m = batch * seq
    steps = batch * n_heads
    assert side_a.shape == side_b.shape and side_a.shape[0] % (16 * steps) == 0
    srows, scols = side_a.shape[0] // steps, side_a.shape[1]
    slab = pl.BlockSpec((srows, scols), lambda b, h: (b * n_heads + h, 0))
    kern = functools.partial(_diff_attn_kernel, tq=tq, seq=seq, lambda_init=lambda_init)
    vec = lambda n: pl.BlockSpec((1, n), lambda b, h: (0, 0))
    return pl.pallas_call(
        kern,
        grid=(batch, n_heads),
        in_specs=[
            pl.BlockSpec((seq, dv), lambda b, h: (b, h)),
            pl.BlockSpec((seq, dv), lambda b, h: (b, n_heads + h)),
            pl.BlockSpec((dv, seq), lambda b, h: (h, b)),
            vec(DIFF_HEAD_DIM), vec(DIFF_HEAD_DIM), vec(DIFF_HEAD_DIM), vec(DIFF_HEAD_DIM),
            pl.BlockSpec((dv, 1), lambda b, h: (0, 0)),
            slab, slab,
        ],
        out_specs=[pl.BlockSpec((seq, dv), lambda b, h: (b, h)), slab, slab],
        out_shape=[jax.ShapeDtypeStruct((m, n_heads * dv), BF16),
                   jax.ShapeDtypeStruct(side_a.shape, BF16), jax.ShapeDtypeStruct(side_b.shape, BF16)],
        scratch_shapes=[pltpu.VMEM((dv, tq), F32), pltpu.VMEM((dv, tq), F32)],
        compiler_params=pltpu.CompilerParams(
            dimension_semantics=("parallel", "parallel"), vmem_limit_bytes=VMEM_LIMIT),
        name="diff_attn",
    )(qk, qk, vt, lq1.reshape(1, -1), lk1.reshape(1, -1), lq2.reshape(1, -1),
      lk2.reshape(1, -1), norm_w.reshape(-1, 1), side_a, side_b)


def _layer_norm(y, g, b):
    mu = jnp.mean(y, -1, keepdims=True)
    yc = y - mu
    var = jnp.mean(yc * yc, -1, keepdims=True)
    return yc * lax.rsqrt(var + 1e-5) * g + b


LN_ROWS = 64


def _deepnorm_epilogue(x_ref, acc_ref, g_ref, beta_ref, alpha, out_refs):
    g = g_ref[...]
    beta = beta_ref[...]

    def rows(i, carry):
        r0 = pl.multiple_of(i * LN_ROWS, LN_ROWS)
        sl = pl.ds(r0, LN_ROWS)
        y = _layer_norm(alpha * x_ref[sl, :] + acc_ref[sl, :], g, beta)
        for o in out_refs:
            o[sl, :] = y.astype(o.dtype)
        return carry

    lax.fori_loop(0, x_ref.shape[0] // LN_ROWS, rows, 0)


def _out_proj_kernel(a_ref, b_ref, w_ref, x_ref, g_ref, beta_ref, o_ref, ob_ref, *, nk, alpha):
    k = pl.program_id(1)
    half = nk // 2

    @pl.when(k == 0)
    def _():
        o_ref[...] = jnp.zeros_like(o_ref)

    def accumulate(lhs_ref):
        n = o_ref.shape[1]
        nb = min(n, 512)
        lhs = lhs_ref[...]
        for n0 in range(0, n, nb):
            o_ref[:, n0:n0 + nb] += jnp.dot(lhs, w_ref[:, n0:n0 + nb], preferred_element_type=F32)

    @pl.when(k < half)
    def _():
        accumulate(a_ref)

    @pl.when(k >= half)
    def _():
        accumulate(b_ref)

    @pl.when(k == nk - 1)
    def _():
        _deepnorm_epilogue(x_ref, o_ref, g_ref, beta_ref, alpha, (o_ref, ob_ref))


def _out_proj_ln(a, b, w, x, g, beta, *, bm, bk, alpha):
    m, ka = a.shape
    n = w.shape[1]
    half = ka // bk
    nk = 2 * half
    kern = functools.partial(_out_proj_kernel, nk=nk, alpha=alpha)
    return pl.pallas_call(
        kern,
        grid=(m // bm, nk),
        in_specs=[
            pl.BlockSpec((bm, bk), lambda i, k: (i, jnp.minimum(k, half - 1))),
            pl.BlockSpec((bm, bk), lambda i, k: (i, jnp.maximum(k - half, 0))),
            pl.BlockSpec((bk, n), lambda i, k: (k, 0)),
            pl.BlockSpec((bm, n), lambda i, k: (i, 0)),
            pl.BlockSpec((1, n), lambda i, k: (0, 0)),
            pl.BlockSpec((1, n), lambda i, k: (0, 0)),
        ],
        out_specs=[pl.BlockSpec((bm, n), lambda i, k: (i, 0)),
                   pl.BlockSpec((bm, n), lambda i, k: (i, 0))],
        out_shape=[jax.ShapeDtypeStruct((m, n), F32), jax.ShapeDtypeStruct((m, n), BF16)],
        compiler_params=pltpu.CompilerParams(
            dimension_semantics=("parallel", "arbitrary"), vmem_limit_bytes=VMEM_LIMIT),
        name="out_proj_ln",
    )(a, b, w, x, g.reshape(1, n), beta.reshape(1, n))


def _ffn_kernel(xb_ref, wg_ref, wu_ref, wd_ref, x_ref, g_ref, beta_ref, o_ref, *, nf, alpha):
    f = pl.program_id(1)

    @pl.when(f == 0)
    def _():
        o_ref[...] = jnp.zeros_like(o_ref)

    xb = xb_ref[...]
    gate = jnp.dot(xb, wg_ref[...], preferred_element_type=F32)
    up = jnp.dot(xb, wu_ref[...], preferred_element_type=F32)
    hid = (_silu(gate) * up).astype(BF16)
    dm = o_ref.shape[1]
    nb = min(dm, 512)
    for n0 in range(0, dm, nb):
        o_ref[:, n0:n0 + nb] += jnp.dot(hid, wd_ref[:, n0:n0 + nb], preferred_element_type=F32)

    @pl.when(f == nf - 1)
    def _():
        _deepnorm_epilogue(x_ref, o_ref, g_ref, beta_ref, alpha, (o_ref,))


def _ffn_ln(xb, x, wg, wu, wd, g, beta, *, bm, bf, alpha):
    m, dm = xb.shape
    ff = wg.shape[1]
    nf = ff // bf
    assert ff % bf == 0
    kern = functools.partial(_ffn_kernel, nf=nf, alpha=alpha)
    return pl.pallas_call(
        kern,
        grid=(m // bm, nf),
        in_specs=[
            pl.BlockSpec((bm, dm), lambda i, f: (i, 0)),
            pl.BlockSpec((dm, bf), lambda i, f: (0, f)),
            pl.BlockSpec((dm, bf), lambda i, f: (0, f)),
            pl.BlockSpec((bf, dm), lambda i, f: (f, 0)),
            pl.BlockSpec((bm, dm), lambda i, f: (i, 0)),
            pl.BlockSpec((1, dm), lambda i, f: (0, 0)),
            pl.BlockSpec((1, dm), lambda i, f: (0, 0)),
        ],
        out_specs=pl.BlockSpec((bm, dm), lambda i, f: (i, 0)),
        out_shape=jax.ShapeDtypeStruct((m, dm), F32),
        compiler_params=pltpu.CompilerParams(
            dimension_semantics=("parallel", "arbitrary"), vmem_limit_bytes=VMEM_LIMIT),
        name="ffn_ln",
    )(xb, wg, wu, wd, x, g.reshape(1, dm), beta.reshape(1, dm))


def _layer(x2, batch, seq, l, w_in, conv_w, a_log, dt_bias, gdn_norm_w, lq1, lk1, lq2, lk2,
           diff_norm_w, w_out, ln1_g, ln1_b, w_gate, w_up, w_down, ln2_g, ln2_b, alpha):
    m, dm = x2.shape
    n_gdn_heads = a_log.shape[0]
    gdn_w = n_gdn_heads * GDN_HEAD_DIM
    diff_w = w_out.shape[0] - gdn_w
    n_diff_heads = diff_w // DIFF_V_DIM
    lambda_init = 0.8 - 0.6 * math.exp(-0.3 * l)
    nc = seq // CHUNK

    bn = 768
    main = 4 * gdn_w
    ab = 2 * n_gdn_heads
    wide = -(-(main + ab) // bn) * bn
    w_t = w_in.T
    xb = x2.astype(BF16)

    proj_g = _matmul_wt32(xb, w_t, 0, wide, 1024, bn, F32, "proj_gdn")
    qk_w = 2 * n_diff_heads * 2 * DIFF_HEAD_DIM
    proj_qk, wd_b = _matmul_wt32(xb, w_t, main + ab, qk_w, 1024, 512, BF16, "proj_qk",
                                 side=(w_down, 256))
    proj_vt, wo_b = _matmul_wt32(xb, w_t, main + ab + qk_w, diff_w, 1024, 512, BF16, "proj_vt",
                                 transposed_out=True, side=(w_out, 128))

    gates = proj_g[:, main:main + ab].reshape(batch, seq, ab).transpose(0, 2, 1)
    gates = gates.reshape(batch, ab, nc, CHUNK)
    gates = jnp.pad(gates, ((0, 0), (0, 0), (0, 0), (0, LANES - CHUNK)))
    gdn_out = _gdn(proj_g, conv_w, gates[:, :n_gdn_heads], gates[:, n_gdn_heads:], a_log, dt_bias,
                   gdn_norm_w, batch=batch, seq=seq, n_heads=n_gdn_heads, heads=2)

    diff_out, wg_b, wu_b = _diff_attn(proj_qk, proj_vt, lq1, lk1, lq2, lk2, diff_norm_w, w_gate, w_up,
                                      batch=batch, seq=seq, n_heads=n_diff_heads, tq=512,
                                      lambda_init=lambda_init)

    x1, x1b = _out_proj_ln(gdn_out, diff_out, wo_b, x2, ln1_g, ln1_b, bm=512, bk=512, alpha=alpha)
    return _ffn_ln(x1b, x1, wg_b, wu_b, wd_b, ln2_g, ln2_b, bm=512, bf=256, alpha=alpha)


def kernel(x, w_in, conv_w, a_log, dt_bias, gdn_norm_w, lambda_q1, lambda_k1, lambda_q2, lambda_k2,
           diff_norm_w, w_out, ln1_g, ln1_b, w_gate, w_up, w_down, ln2_g, ln2_b):
    batch, seq, dm = x.shape
    depth = w_in.shape[0]
    alpha = (2 * depth) ** 0.25
    x2 = x.reshape(batch * seq, dm)
    for l in range(depth):
        x2 = _layer(x2, batch, seq, l, w_in[l], conv_w[l], a_log[l], dt_bias[l], gdn_norm_w[l],
                    lambda_q1[l], lambda_k1[l], lambda_q2[l], lambda_k2[l], diff_norm_w[l], w_out[l],
                    ln1_g[l], ln1_b[l], w_gate[l], w_up[l], w_down[l], ln2_g[l], ln2_b[l], alpha)
    return x2.reshape(batch, seq, dm)
```

```python
import functools
import math

import jax
import jax.numpy as jnp
from jax import lax
from jax.experimental import pallas as pl
from jax.experimental.pallas import tpu as pltpu

F32 = jnp.float32
BF16 = jnp.bfloat16

GDN_HEAD_DIM = 128
DIFF_HEAD_DIM = 128
DIFF_V_DIM = 256
CONV_K = 4
CHUNK = 64
PROLOGUE_ROWS = 128
LANES = 128
VMEM_LIMIT = 60 * 1024 * 1024


def _dot(a, b):
    return jnp.dot(a.astype(BF16), b.astype(BF16), preferred_element_type=F32)


def _dot_nt(a, b):
    return lax.dot_general(a.astype(BF16), b.astype(BF16), (((1,), (1,)), ((), ())),
                           preferred_element_type=F32)


def _dot_tn(a, b):
    return lax.dot_general(a.astype(BF16), b.astype(BF16), (((0,), (0,)), ((), ())),
                           preferred_element_type=F32)


def _silu(x):
    return x * jax.nn.sigmoid(x)


CAST_ROWS = 128


SIDE_COLS = 1024


def _cast_block(src_ref, dst_ref, col0=0, col1=None):
    col1 = src_ref.shape[1] if col1 is None else col1
    for c0 in range(col0, col1, SIDE_COLS):
        c1 = min(c0 + SIDE_COLS, col1)
        dst_ref[:, c0:c1] = src_ref[:, c0:c1].astype(BF16)


def _mm_wt32_kernel(x_ref, w_ref, *rest, transposed_out, has_side):
    if has_side:
        side_ref, o_ref, side_out_ref, wb_ref = rest
        _cast_block(side_ref, side_out_ref)
    else:
        o_ref, wb_ref = rest

    @pl.when(pl.program_id(1) == 0)
    def _():
        for r in range(0, w_ref.shape[0], CAST_ROWS):
            wb_ref[r:r + CAST_ROWS, :] = w_ref[r:r + CAST_ROWS, :].astype(BF16)

    if transposed_out:
        o_ref[...] = _dot_nt(wb_ref[...], x_ref[...]).astype(o_ref.dtype)
    else:
        o_ref[...] = _dot_nt(x_ref[...], wb_ref[...]).astype(o_ref.dtype)


def _matmul_wt32(x, w_t, row0, n_out, bm, bn, out_dtype, name, transposed_out=False, side=None):
    m, k = x.shape
    assert m % bm == 0 and n_out % bn == 0 and bn % CAST_ROWS == 0 and row0 % 8 == 0
    assert row0 + n_out <= w_t.shape[0] and k == w_t.shape[1]
    nj, ni = n_out // bn, m // bm
    if transposed_out:
        out_specs = [pl.BlockSpec((bn, bm), lambda j, i: (j, i))]
        out_shape = [jax.ShapeDtypeStruct((n_out, m), out_dtype)]
    else:
        out_specs = [pl.BlockSpec((bm, bn), lambda j, i: (i, j))]
        out_shape = [jax.ShapeDtypeStruct((m, n_out), out_dtype)]
    in_specs = [pl.BlockSpec((bm, k), lambda j, i: (i, 0)),
                pl.BlockSpec((pl.Element(bn), pl.Element(k)), lambda j, i: ((row0 // 8 + j * (bn // 8)) * 8, 0))]
    args = [x, w_t]
    if side is not None:
        a, rows = side
        n_slabs = a.shape[0] // rows
        assert a.shape[0] % rows == 0 and rows % 16 == 0 and n_slabs <= nj * ni
        slab = lambda j, i: (jnp.minimum(j * ni + i, n_slabs - 1), 0)
        in_specs.append(pl.BlockSpec((rows, a.shape[1]), slab))
        out_specs.append(pl.BlockSpec((rows, a.shape[1]), slab))
        out_shape.append(jax.ShapeDtypeStruct(a.shape, BF16))
        args.append(a)
    outs = pl.pallas_call(
        functools.partial(_mm_wt32_kernel, transposed_out=transposed_out, has_side=side is not None),
        grid=(nj, ni),
        in_specs=in_specs,
        out_specs=out_specs,
        out_shape=out_shape,
        scratch_shapes=[pltpu.VMEM((bn, k), BF16)],
        compiler_params=pltpu.CompilerParams(
            dimension_semantics=("arbitrary", "arbitrary"), vmem_limit_bytes=VMEM_LIMIT),
        name=name,
    )(*args)
    return outs if side is not None else outs[0]


def _gdn_kernel(q_ref, k_ref, v_ref, z_ref, cwq_ref, cwk_ref, cwv_ref, ga_ref, gb_ref,
                alog_ref, dtb_ref, nw_ref, o_ref,
                xpad, qs, ks, vs, q2s, k2s, o0s, rs, gcs, bts, egs, *, heads, seq, unroll1, unroll2):
    d = GDN_HEAD_DIM
    c_ = CHUNK
    nc = seq // c_
    row = lax.broadcasted_iota(jnp.int32, (c_, c_), 0)
    col = lax.broadcasted_iota(jnp.int32, (c_, c_), 1)
    tril = col <= row
    strict = col < row
    eye = col == row
    col2 = lax.broadcasted_iota(jnp.int32, (c_, 2 * c_), 1)
    right = col2 >= c_
    eye_r = jnp.where(col2 == lax.broadcasted_iota(jnp.int32, (c_, 2 * c_), 0) + c_, 1.0, 0.0)
    lane = lax.broadcasted_iota(jnp.int32, (nc, LANES), 1)

    rb = min(seq, PROLOGUE_ROWS)
    halo = CONV_K - 1

    def conv_silu(x_ref, cw_ref, dst, g, post):
        cols = slice(g * d, (g + 1) * d)
        w = cw_ref[:, cols]
        xpad[0:8, :] = jnp.zeros((8, d), F32)
        xpad[8:8 + rb, :] = x_ref[0:rb, cols]
        for r0 in range(0, seq, rb):
            acc = None
            for j in range(CONV_K):
                if r0 == 0:
                    xs = xpad[8 - halo + j:8 - halo + j + rb, :]
                else:
                    xs = x_ref[r0 - halo + j:r0 - halo + j + rb, cols]
                t = xs * w[j:j + 1, :]
                acc = t if acc is None else acc + t
            dst[g, r0:r0 + rb, :] = post(_silu(acc))

    l2n = lambda y: y * lax.rsqrt(jnp.sum(y * y, -1, keepdims=True) + 1e-6)

    for g in range(heads):
        conv_silu(q_ref, cwq_ref, qs, g, lambda y: l2n(y) * (d ** -0.5))
        conv_silu(k_ref, cwk_ref, ks, g, l2n)
        conv_silu(v_ref, cwv_ref, vs, g, lambda y: y)

        a = ga_ref[0, g] + dtb_ref[g]
        sp = jnp.maximum(a, 0.0) + jnp.log1p(jnp.exp(-jnp.abs(a)))
        gl = jnp.where(lane < c_, -jnp.exp(alog_ref[g]) * sp, 0.0)
        sh = 1
        while sh < c_:
            gl = gl + jnp.where(lane >= sh, pltpu.roll(gl, sh, 1), 0.0)
            sh *= 2
        gcs[g] = gl
        bts[g] = jax.nn.sigmoid(gb_ref[0, g])

    def phase1(cg, carry):
        chains = [(cg * unroll1 + j, g) for j in range(unroll1) for g in range(heads)]
        st = []
        for c, g in chains:
            r0 = pl.multiple_of(c * c_, c_)
            qc = qs[g, pl.ds(r0, c_), :]
            kc = ks[g, pl.ds(r0, c_), :]
            vc = vs[g, pl.ds(r0, c_), :]
            gcr = gcs[g, pl.ds(c, 1), :][:, :c_]
            btr = bts[g, pl.ds(c, 1), :][:, :c_]
            gcol = jnp.sum(jnp.where(eye, jnp.broadcast_to(gcr, (c_, c_)), 0.0), 1, keepdims=True)
            bcol = jnp.sum(jnp.where(eye, jnp.broadcast_to(btr, (c_, c_)), 0.0), 1, keepdims=True)
            glast = gcr[:, c_ - 1:c_]
            decay = jnp.where(tril, jnp.exp(jnp.where(tril, gcol - gcr, 0.0)), 0.0)
            kb = kc * bcol
            r = jnp.concatenate([vc * bcol, kb * jnp.exp(gcol)], axis=1)
            st.append(dict(c=c, g=g, r0=r0, qc=qc, kc=kc, kb=kb, gcol=gcol, glast=glast, decay=decay, r=r))
        for t in st:
            t["aq"] = _dot_nt(jnp.concatenate([t["kb"], t["qc"]], axis=0), t["kc"])
        for t in st:
            t["nl"] = jnp.where(strict, -(t["aq"][:c_] * t["decay"]), 0.0)
            t["intra"] = jnp.where(tril, t["aq"][c_:] * t["decay"], 0.0)
        n_levels = c_.bit_length() - 1
        for t in st:
            nt0 = jnp.concatenate([t["nl"], jnp.zeros((c_, c_), F32)], axis=1) + eye_r
            t["nt"] = _dot(t["nl"], nt0) + eye_r
        for lvl in range(1, n_levels):
            for t in st:
                nt = t["nt"]
                t["nt"] = _dot(nt[:, :c_], nt) + jnp.where(right, nt, 0.0)
        for t in st:
            t["r"] = _dot(t["nt"], jnp.concatenate([jnp.zeros_like(t["r"]), t["r"]], axis=0))
        for t in st:
            kd = t["kc"] * jnp.exp(t["glast"] - t["gcol"])
            t["kq"] = _dot_tn(kd, t["r"])
            t["io"] = _dot(t["intra"], t["r"])
        for t in st:
            c, g, r0 = t["c"], t["g"], t["r0"]
            rd = pl.multiple_of(c * d, d)
            q2s[g, pl.ds(rd, d), :] = t["kq"][:, :d]
            k2s[g, pl.ds(rd, d), :] = t["kq"][:, d:].astype(BF16)
            o0s[g, pl.ds(r0, c_), :] = t["io"][:, :d]
            rs[g, pl.ds(r0, c_), :] = (t["qc"] * jnp.exp(t["gcol"]) - t["io"][:, d:]).astype(BF16)
            egs[g, pl.ds(c, 1), :] = jnp.broadcast_to(jnp.exp(t["glast"]), (1, LANES))
        return carry

    lax.fori_loop(0, nc // unroll1, phase1, 0)

    def phase2(c, states):
        r0 = pl.multiple_of(c * c_, c_)
        rd = pl.multiple_of(c * d, d)
        sbs = [states[g].astype(BF16) for g in range(heads)]
        ks2 = [_dot(k2s[g, pl.ds(rd, d), :], sbs[g]) for g in range(heads)]
        ros = [_dot(rs[g, pl.ds(r0, c_), :], sbs[g]) for g in range(heads)]
        new_states = [states[g] * egs[g, pl.ds(c, 1), :] + q2s[g, pl.ds(rd, d), :] - ks2[g]
                      for g in range(heads)]
        for g in range(heads):
            o0s[g, pl.ds(r0, c_), :] += ros[g]
        return tuple(new_states)

    lax.fori_loop(0, nc, phase2, tuple(jnp.zeros((d, d), F32) for _ in range(heads)), unroll=unroll2)

    nw = nw_ref[...]
    for g in range(heads):
        cols = slice(g * d, (g + 1) * d)
        for r0 in range(0, seq, rb):
            o = o0s[g, r0:r0 + rb, :]
            on = o * lax.rsqrt(jnp.mean(o * o, -1, keepdims=True) + 1e-6) * nw
            o_ref[r0:r0 + rb, cols] = (on * _silu(z_ref[r0:r0 + rb, cols])).astype(o_ref.dtype)


def _gdn(proj, conv_w, ga_rows, gb_rows, a_log, dt_bias, norm_w, *, batch, seq, n_heads, heads):
    d = GDN_HEAD_DIM
    gw = heads * d
    hb = n_heads // heads
    nc = seq // CHUNK
    m = batch * seq
    kern = functools.partial(_gdn_kernel, heads=heads, seq=seq, unroll1=math.gcd(nc, 8), unroll2=2)
    col = lambda off: (lambda b, h: (b, off * hb + h))
    cw = lambda off: (lambda b, h: (0, off * hb + h))
    return pl.pallas_call(
        kern,
        grid=(batch, hb),
        in_specs=[
            pl.BlockSpec((seq, gw), col(0)), pl.BlockSpec((seq, gw), col(1)),
            pl.BlockSpec((seq, gw), col(2)), pl.BlockSpec((seq, gw), col(3)),
            pl.BlockSpec((CONV_K, gw), cw(0)), pl.BlockSpec((CONV_K, gw), cw(1)),
            pl.BlockSpec((CONV_K, gw), cw(2)),
            pl.BlockSpec((1, heads, nc, LANES), lambda b, h: (b, h, 0, 0)),
            pl.BlockSpec((1, heads, nc, LANES), lambda b, h: (b, h, 0, 0)),
            pl.BlockSpec((heads, 1, 1), lambda b, h: (h, 0, 0)),
            pl.BlockSpec((heads, 1, 1), lambda b, h: (h, 0, 0)),
            pl.BlockSpec((1, d), lambda b, h: (0, 0)),
        ],
        out_specs=pl.BlockSpec((seq, gw), lambda b, h: (b, h)),
        out_shape=jax.ShapeDtypeStruct((m, n_heads * d), BF16),
        scratch_shapes=[
            pltpu.VMEM((PROLOGUE_ROWS + 8, d), F32),
            pltpu.VMEM((heads, seq, d), F32),
            pltpu.VMEM((heads, seq, d), F32),
            pltpu.VMEM((heads, seq, d), F32),
            pltpu.VMEM((heads, nc * d, d), F32),
            pltpu.VMEM((heads, nc * d, d), BF16),
            pltpu.VMEM((heads, seq, d), F32),
            pltpu.VMEM((heads, seq, d), BF16),
            pltpu.VMEM((heads, nc, LANES), F32),
            pltpu.VMEM((heads, nc, LANES), F32),
            pltpu.VMEM((heads, nc, LANES), F32),
        ],
        compiler_params=pltpu.CompilerParams(
            dimension_semantics=("parallel", "parallel"), vmem_limit_bytes=VMEM_LIMIT),
        name="gdn",
    )(proj, proj, proj, proj, conv_w, conv_w, conv_w, ga_rows, gb_rows,
      a_log.reshape(n_heads, 1, 1), dt_bias.reshape(n_heads, 1, 1), norm_w.reshape(1, d))


def _diff_attn_kernel(q_ref, k_ref, vt_ref, lq1_ref, lk1_ref, lq2_ref, lk2_ref, nw_ref, sa_ref, sb_ref,
                      o_ref, sa_out_ref, sb_out_ref, acc1, acc2, *, tq, seq, lambda_init):
    dh = DIFF_HEAD_DIM
    c2 = (dh ** -0.5) * math.log2(math.e)
    neg = float(jnp.finfo(jnp.float32).min)
    krow = lax.broadcasted_iota(jnp.int32, (tq, tq), 0)
    qcol = lax.broadcasted_iota(jnp.int32, (tq, tq), 1)
    keep = krow <= qcol
    lam = (jnp.exp(jnp.sum(lq1_ref[...] * lk1_ref[...], -1, keepdims=True))
           - jnp.exp(jnp.sum(lq2_ref[...] * lk2_ref[...], -1, keepdims=True)) + lambda_init)
    nw = nw_ref[...] * (1.0 - lambda_init)

    nq = seq // tq
    side_cols = sa_ref.shape[1]
    piece = -(-side_cols // (nq * LANES)) * LANES
    for i in range(nq):
        for src, dst in ((sa_ref, sa_out_ref), (sb_ref, sb_out_ref)):
            if i * piece < side_cols:
                _cast_block(src, dst, i * piece, min((i + 1) * piece, side_cols))
        q1 = q_ref[i * tq:(i + 1) * tq, :dh]
        q2 = q_ref[i * tq:(i + 1) * tq, dh:]
        m1 = m2 = jnp.full((1, tq), neg, F32)
        l1 = l2 = jnp.zeros((1, tq), F32)
        for j in range(i + 1):
            kj = k_ref[j * tq:(j + 1) * tq, :]
            vt = vt_ref[:, j * tq:(j + 1) * tq]
            s1 = _dot_nt(kj[:, :dh], q1)
            s2 = _dot_nt(kj[:, dh:], q2)
            if j == i:
                s1 = jnp.where(keep, s1, neg)
                s2 = jnp.where(keep, s2, neg)
            m1n = jnp.maximum(m1, jnp.max(s1, 0, keepdims=True))
            m2n = jnp.maximum(m2, jnp.max(s2, 0, keepdims=True))
            a1 = jnp.exp2((m1 - m1n) * c2)
            a2 = jnp.exp2((m2 - m2n) * c2)
            p1 = jnp.exp2((s1 - m1n) * c2)
            p2 = jnp.exp2((s2 - m2n) * c2)
            l1 = a1 * l1 + jnp.sum(p1, 0, keepdims=True)
            l2 = a2 * l2 + jnp.sum(p2, 0, keepdims=True)
            pv1 = _dot(vt, p1)
            pv2 = _dot(vt, p2)
            if j == 0:
                acc1[...] = pv1
                acc2[...] = pv2
            else:
                acc1[...] = acc1[...] * a1 + pv1
                acc2[...] = acc2[...] * a2 + pv2
            m1, m2 = m1n, m2n
        o = acc1[...] / l1 - lam * (acc2[...] / l2)
        on = o * lax.rsqrt(jnp.mean(o * o, 0, keepdims=True) + 1e-6) * nw
        o_ref[i * tq:(i + 1) * tq, :] = on.T.astype(o_ref.dtype)


def _diff_attn(qk, vt, lq1, lk1, lq2, lk2, norm_w, side_a, side_b, *, batch, seq, n_heads, tq, lambda_init):
    dv = DIFF_V_DIM
    m = batch * seq
    steps = batch * n_heads
    assert side_a.shape == side_b.shape and side_a.shape[0] % (16 * steps) == 0
    srows, scols = side_a.shape[0] // steps, side_a.shape[1]
    slab = pl.BlockSpec((srows, scols), lambda b, h: (b * n_heads + h, 0))
    kern = functools.partial(_diff_attn_kernel, tq=tq, seq=seq, lambda_init=lambda_init)
    vec = lambda n: pl.BlockSpec((1, n), lambda b, h: (0, 0))
    return pl.pallas_call(
        kern,
        grid=(batch, n_heads),
        in_specs=[
            pl.BlockSpec((seq, dv), lambda b, h: (b, h)),
            pl.BlockSpec((seq, dv), lambda b, h: (b, n_heads + h)),
            pl.BlockSpec((dv, seq), lambda b, h: (h, b)),
            vec(DIFF_HEAD_DIM), vec(DIFF_HEAD_DIM), vec(DIFF_HEAD_DIM), vec(DIFF_HEAD_DIM),
            pl.BlockSpec((dv, 1), lambda b, h: (0, 0)),
            slab, slab,
        ],
        out_specs=[pl.BlockSpec((seq, dv), lambda b, h: (b, h)), slab, slab],
        out_shape=[jax.ShapeDtypeStruct((m, n_heads * dv), BF16),
                   jax.ShapeDtypeStruct(side_a.shape, BF16), jax.ShapeDtypeStruct(side_b.shape, BF16)],
        scratch_shapes=[pltpu.VMEM((dv, tq), F32), pltpu.VMEM((dv, tq), F32)],
        compiler_params=pltpu.CompilerParams(
            dimension_semantics=("parallel", "parallel"), vmem_limit_bytes=VMEM_LIMIT),
        name="diff_attn",
    )(qk, qk, vt, lq1.reshape(1, -1), lk1.reshape(1, -1), lq2.reshape(1, -1),
      lk2.reshape(1, -1), norm_w.reshape(-1, 1), side_a, side_b)


def _layer_norm(y, g, b):
    mu = jnp.mean(y, -1, keepdims=True)
    yc = y - mu
    var = jnp.mean(yc * yc, -1, keepdims=True)
    return yc * lax.rsqrt(var + 1e-5) * g + b


LN_ROWS = 64


def _deepnorm_epilogue(x_ref, acc_ref, g_ref, beta_ref, alpha, out_refs):
    g = g_ref[...]
    beta = beta_ref[...]

    def rows(i, carry):
        r0 = pl.multiple_of(i * LN_ROWS, LN_ROWS)
        sl = pl.ds(r0, LN_ROWS)
        y = _layer_norm(alpha * x_ref[sl, :] + acc_ref[sl, :], g, beta)
        for o in out_refs:
            o[sl, :] = y.astype(o.dtype)
        return carry

    lax.fori_loop(0, x_ref.shape[0] // LN_ROWS, rows, 0)


def _out_proj_kernel(a_ref, b_ref, w_ref, x_ref, g_ref, beta_ref, o_ref, ob_ref, *, nk, alpha):
    k = pl.program_id(1)
    half = nk // 2

    @pl.when(k == 0)
    def _():
        o_ref[...] = jnp.zeros_like(o_ref)

    def accumulate(lhs_ref):
        n = o_ref.shape[1]
        nb = min(n, 512)
        lhs = lhs_ref[...]
        for n0 in range(0, n, nb):
            o_ref[:, n0:n0 + nb] += jnp.dot(lhs, w_ref[:, n0:n0 + nb], preferred_element_type=F32)

    @pl.when(k < half)
    def _():
        accumulate(a_ref)

    @pl.when(k >= half)
    def _():
        accumulate(b_ref)

    @pl.when(k == nk - 1)
    def _():
        _deepnorm_epilogue(x_ref, o_ref, g_ref, beta_ref, alpha, (o_ref, ob_ref))


def _out_proj_ln(a, b, w, x, g, beta, *, bm, bk, alpha):
    m, ka = a.shape
    n = w.shape[1]
    half = ka // bk
    nk = 2 * half
    kern = functools.partial(_out_proj_kernel, nk=nk, alpha=alpha)
    return pl.pallas_call(
        kern,
        grid=(m // bm, nk),
        in_specs=[
            pl.BlockSpec((bm, bk), lambda i, k: (i, jnp.minimum(k, half - 1))),
            pl.BlockSpec((bm, bk), lambda i, k: (i, jnp.maximum(k - half, 0))),
            pl.BlockSpec((bk, n), lambda i, k: (k, 0)),
            pl.BlockSpec((bm, n), lambda i, k: (i, 0)),
            pl.BlockSpec((1, n), lambda i, k: (0, 0)),
            pl.BlockSpec((1, n), lambda i, k: (0, 0)),
        ],
        out_specs=[pl.BlockSpec((bm, n), lambda i, k: (i, 0)),
                   pl.BlockSpec((bm, n), lambda i, k: (i, 0))],
        out_shape=[jax.ShapeDtypeStruct((m, n), F32), jax.ShapeDtypeStruct((m, n), BF16)],
        compiler_params=pltpu.CompilerParams(
            dimension_semantics=("parallel", "arbitrary"), vmem_limit_bytes=VMEM_LIMIT),
        name="out_proj_ln",
    )(a, b, w, x, g.reshape(1, n), beta.reshape(1, n))


class _DeepNormStream:
    def __init__(self, x_hbm, out_hbms, row0, acc_ref, xin, youts, sem_in, sem_out):
        self.x_hbm, self.out_hbms, self.row0, self.acc_ref = x_hbm, out_hbms, row0, acc_ref
        self.xin, self.youts, self.sem_in, self.sem_out = xin, youts, sem_in, sem_out
        self.n = acc_ref.shape[0] // LN_ROWS
        assert acc_ref.shape[0] % LN_ROWS == 0 and self.n >= 2

    def _rows(self, c):
        return pl.ds(pl.multiple_of(self.row0 + c * LN_ROWS, LN_ROWS), LN_ROWS)

    def in_copy(self, c, slot):
        return pltpu.make_async_copy(self.x_hbm.at[self._rows(c), :], self.xin.at[slot], self.sem_in.at[slot])

    def out_copy(self, k, c, slot):
        return pltpu.make_async_copy(self.youts[k].at[slot], self.out_hbms[k].at[self._rows(c), :],
                                     self.sem_out.at[k, slot])

    def prefetch(self):
        self.in_copy(0, 0).start()

    def run(self, g_ref, beta_ref, alpha):
        g = g_ref[...]
        beta = beta_ref[...]
        n, nout = self.n, len(self.out_hbms)

        def chunk(c, carry):
            slot = c % 2
            self.in_copy(c, slot).wait()

            @pl.when(c + 1 < n)
            def _():
                self.in_copy(c + 1, 1 - slot).start()

            @pl.when(c >= 2)
            def _():
                for k in range(nout):
                    self.out_copy(k, c - 2, slot).wait()

            r0 = pl.multiple_of(c * LN_ROWS, LN_ROWS)
            y = _layer_norm(alpha * self.xin[slot] + self.acc_ref[pl.ds(r0, LN_ROWS), :], g, beta)
            for k in range(nout):
                self.youts[k][slot] = y.astype(self.youts[k].dtype)
                self.out_copy(k, c, slot).start()
            return carry

        lax.fori_loop(0, n, chunk, 0)
        for c in (n - 2, n - 1):
            for k in range(nout):
                self.out_copy(k, c, c % 2).wait()


def _ffn_kernel(xb_ref, wg_ref, wu_ref, wd_ref, g_ref, beta_ref, x_hbm, o_hbm,
                acc, xin, yout, sem_in, sem_out, *, nf, alpha):
    i = pl.program_id(0)
    f = pl.program_id(1)
    bm, dm = acc.shape
    stream = _DeepNormStream(x_hbm, (o_hbm,), i * bm, acc, xin, (yout,), sem_in, sem_out)

    @pl.when(f == nf - 1)
    def _():
        stream.prefetch()

    xb = xb_ref[...]
    gate = jnp.dot(xb, wg_ref[...], preferred_element_type=F32)
    up = jnp.dot(xb, wu_ref[...], preferred_element_type=F32)
    hid = (_silu(gate) * up).astype(BF16)
    nb = min(dm, 512)

    @pl.when(f == 0)
    def _():
        for n0 in range(0, dm, nb):
            acc[:, n0:n0 + nb] = jnp.dot(hid, wd_ref[:, n0:n0 + nb], preferred_element_type=F32)

    @pl.when(f > 0)
    def _():
        for n0 in range(0, dm, nb):
            acc[:, n0:n0 + nb] += jnp.dot(hid, wd_ref[:, n0:n0 + nb], preferred_element_type=F32)

    @pl.when(f == nf - 1)
    def _():
        stream.run(g_ref, beta_ref, alpha)


def _ffn_ln(xb, x, wg, wu, wd, g, beta, *, bm, bf, alpha):
    m, dm = xb.shape
    ff = wg.shape[1]
    nf = ff // bf
    assert ff % bf == 0 and m % bm == 0
    kern = functools.partial(_ffn_kernel, nf=nf, alpha=alpha)
    return pl.pallas_call(
        kern,
        grid=(m // bm, nf),
        in_specs=[
            pl.BlockSpec((bm, dm), lambda i, f: (i, 0)),
            pl.BlockSpec((dm, bf), lambda i, f: (0, f)),
            pl.BlockSpec((dm, bf), lambda i, f: (0, f)),
            pl.BlockSpec((bf, dm), lambda i, f: (f, 0)),
            pl.BlockSpec((1, dm), lambda i, f: (0, 0)),
            pl.BlockSpec((1, dm), lambda i, f: (0, 0)),
            pl.BlockSpec(memory_space=pl.ANY),
        ],
        out_specs=pl.BlockSpec(memory_space=pl.ANY),
        out_shape=jax.ShapeDtypeStruct((m, dm), F32),
        scratch_shapes=[
            pltpu.VMEM((bm, dm), F32),
            pltpu.VMEM((2, LN_ROWS, dm), F32),
            pltpu.VMEM((2, LN_ROWS, dm), F32),
            pltpu.SemaphoreType.DMA((2,)),
            pltpu.SemaphoreType.DMA((1, 2)),
        ],
        compiler_params=pltpu.CompilerParams(
            dimension_semantics=("arbitrary", "arbitrary"), vmem_limit_bytes=VMEM_LIMIT),
        name="ffn_ln",
    )(xb, wg, wu, wd, g.reshape(1, dm), beta.reshape(1, dm), x)


def _layer(x2, batch, seq, l, w_in, conv_w, a_log, dt_bias, gdn_norm_w, lq1, lk1, lq2, lk2,
           diff_norm_w, w_out, ln1_g, ln1_b, w_gate, w_up, w_down, ln2_g, ln2_b, alpha):
    m, dm = x2.shape
    n_gdn_heads = a_log.shape[0]
    gdn_w = n_gdn_heads * GDN_HEAD_DIM
    diff_w = w_out.shape[0] - gdn_w
    n_diff_heads = diff_w // DIFF_V_DIM
    lambda_init = 0.8 - 0.6 * math.exp(-0.3 * l)
    nc = seq // CHUNK

    bn = 768
    main = 4 * gdn_w
    ab = 2 * n_gdn_heads
    wide = -(-(main + ab) // bn) * bn
    w_t = w_in.T
    xb = x2.astype(BF16)

    proj_g = _matmul_wt32(xb, w_t, 0, wide, 1024, bn, F32, "proj_gdn")
    qk_w = 2 * n_diff_heads * 2 * DIFF_HEAD_DIM
    proj_qk, wd_b = _matmul_wt32(xb, w_t, main + ab, qk_w, 1024, 512, BF16, "proj_qk",
                                 side=(w_down, 256))
    proj_vt, wo_b = _matmul_wt32(xb, w_t, main + ab + qk_w, diff_w, 1024, 512, BF16, "proj_vt",
                                 transposed_out=True, side=(w_out, 128))

    gates = proj_g[:, main:main + ab].reshape(batch, seq, ab).transpose(0, 2, 1)
    gates = gates.reshape(batch, ab, nc, CHUNK)
    gates = jnp.pad(gates, ((0, 0), (0, 0), (0, 0), (0, LANES - CHUNK)))
    gdn_out = _gdn(proj_g, conv_w, gates[:, :n_gdn_heads], gates[:, n_gdn_heads:], a_log, dt_bias,
                   gdn_norm_w, batch=batch, seq=seq, n_heads=n_gdn_heads, heads=2)

    diff_out, wg_b, wu_b = _diff_attn(proj_qk, proj_vt, lq1, lk1, lq2, lk2, diff_norm_w, w_gate, w_up,
                                      batch=batch, seq=seq, n_heads=n_diff_heads, tq=512,
                                      lambda_init=lambda_init)

    x1, x1b = _out_proj_ln(gdn_out, diff_out, wo_b, x2, ln1_g, ln1_b, bm=512, bk=512, alpha=alpha)
    return _ffn_ln(x1b, x1, wg_b, wu_b, wd_b, ln2_g, ln2_b, bm=1024, bf=256, alpha=alpha)


def kernel(x, w_in, conv_w, a_log, dt_bias, gdn_norm_w, lambda_q1, lambda_k1, lambda_q2, lambda_k2,
           diff_norm_w, w_out, ln1_g, ln1_b, w_gate, w_up, w_down, ln2_g, ln2_b):
    batch, seq, dm = x.shape
    depth = w_in.shape[0]
    alpha = (2 * depth) ** 0.25
    x2 = x.reshape(batch * seq, dm)
    for l in range(depth):
        x2 = _layer(x2, batch, seq, l, w_in[l], conv_w[l], a_log[l], dt_bias[l], gdn_norm_w[l],
                    lambda_q1[l], lambda_k1[l], lambda_q2[l], lambda_k2[l], diff_norm_w[l], w_out[l],
                    ln1_g[l], ln1_b[l], w_gate[l], w_up[l], w_down[l], ln2_g[l], ln2_b[l], alpha)
    return x2.reshape(batch, seq, dm)
```

```python
import functools
import math

import jax
import jax.numpy as jnp
from jax import lax
from jax.experimental import pallas as pl
from jax.experimental.pallas import tpu as pltpu

F32 = jnp.float32
BF16 = jnp.bfloat16

GDN_HEAD_DIM = 128
DIFF_HEAD_DIM = 128
DIFF_V_DIM = 256
CONV_K = 4
CHUNK = 64
PROLOGUE_ROWS = 128
LANES = 128
VMEM_LIMIT = 60 * 1024 * 1024


def _dot(a, b):
    return jnp.dot(a.astype(BF16), b.astype(BF16), preferred_element_type=F32)


def _dot_nt(a, b):
    return lax.dot_general(a.astype(BF16), b.astype(BF16), (((1,), (1,)), ((), ())),
                           preferred_element_type=F32)


def _dot_tn(a, b):
    return lax.dot_general(a.astype(BF16), b.astype(BF16), (((0,), (0,)), ((), ())),
                           preferred_element_type=F32)


def _silu(x):
    return x * jax.nn.sigmoid(x)


CAST_ROWS = 128
CAST_PIECE = 256


SIDE_COLS = 1024


def _cast_block(src_ref, dst_ref, col0=0, col1=None):
    col1 = src_ref.shape[1] if col1 is None else col1
    for c0 in range(col0, col1, SIDE_COLS):
        c1 = min(c0 + SIDE_COLS, col1)
        dst_ref[:, c0:c1] = src_ref[:, c0:c1].astype(BF16)


def _mm_wt32_kernel(x_ref, w_ref, *rest, transposed_out, has_side):
    if has_side:
        side_ref, o_ref, side_out_ref, wb_ref = rest
        _cast_block(side_ref, side_out_ref)
    else:
        o_ref, wb_ref = rest

    def product(rows):
        if transposed_out:
            o_ref[rows, :] = _dot_nt(wb_ref[rows, :], x_ref[...]).astype(o_ref.dtype)
        else:
            o_ref[:, rows] = _dot_nt(x_ref[...], wb_ref[rows, :]).astype(o_ref.dtype)

    @pl.when(pl.program_id(1) == 0)
    def _():
        for p0 in range(0, w_ref.shape[0], CAST_PIECE):
            for r in range(p0, p0 + CAST_PIECE, CAST_ROWS):
                wb_ref[r:r + CAST_ROWS, :] = w_ref[r:r + CAST_ROWS, :].astype(BF16)
            product(slice(p0, p0 + CAST_PIECE))

    @pl.when(pl.program_id(1) != 0)
    def _():
        product(slice(None))


def _matmul_wt32(x, w_t, row0, n_out, bm, bn, out_dtype, name, transposed_out=False, side=None):
    m, k = x.shape
    assert m % bm == 0 and n_out % bn == 0 and bn % CAST_PIECE == 0 and CAST_PIECE % CAST_ROWS == 0
    assert row0 % 8 == 0
    assert row0 + n_out <= w_t.shape[0] and k == w_t.shape[1]
    nj, ni = n_out // bn, m // bm
    if transposed_out:
        out_specs = [pl.BlockSpec((bn, bm), lambda j, i: (j, i))]
        out_shape = [jax.ShapeDtypeStruct((n_out, m), out_dtype)]
    else:
        out_specs = [pl.BlockSpec((bm, bn), lambda j, i: (i, j))]
        out_shape = [jax.ShapeDtypeStruct((m, n_out), out_dtype)]
    in_specs = [pl.BlockSpec((bm, k), lambda j, i: (i, 0)),
                pl.BlockSpec((pl.Element(bn), pl.Element(k)), lambda j, i: ((row0 // 8 + j * (bn // 8)) * 8, 0))]
    args = [x, w_t]
    if side is not None:
        a, rows = side
        n_slabs = a.shape[0] // rows
        assert a.shape[0] % rows == 0 and rows % 16 == 0 and n_slabs <= nj * ni
        slab = lambda j, i: (jnp.minimum(j * ni + i, n_slabs - 1), 0)
        in_specs.append(pl.BlockSpec((rows, a.shape[1]), slab))
        out_specs.append(pl.BlockSpec((rows, a.shape[1]), slab))
        out_shape.append(jax.ShapeDtypeStruct(a.shape, BF16))
        args.append(a)
    outs = pl.pallas_call(
        functools.partial(_mm_wt32_kernel, transposed_out=transposed_out, has_side=side is not None),
        grid=(nj, ni),
        in_specs=in_specs,
        out_specs=out_specs,
        out_shape=out_shape,
        scratch_shapes=[pltpu.VMEM((bn, k), BF16)],
        compiler_params=pltpu.CompilerParams(
            dimension_semantics=("arbitrary", "arbitrary"), vmem_limit_bytes=VMEM_LIMIT),
        name=name,
    )(*args)
    return outs if side is not None else outs[0]


def _gdn_kernel(q_ref, k_ref, v_ref, z_ref, cwq_ref, cwk_ref, cwv_ref, ga_ref, gb_ref,
                alog_ref, dtb_ref, nw_ref, o_ref,
                xpad, qs, ks, vs, q2s, k2s, o0s, rs, gcs, bts, egs, *, heads, seq, unroll1, unroll2):
    d = GDN_HEAD_DIM
    c_ = CHUNK
    nc = seq // c_
    row = lax.broadcasted_iota(jnp.int32, (c_, c_), 0)
    col = lax.broadcasted_iota(jnp.int32, (c_, c_), 1)
    tril = col <= row
    strict = col < row
    eye = col == row
    col2 = lax.broadcasted_iota(jnp.int32, (c_, 2 * c_), 1)
    right = col2 >= c_
    eye_r = jnp.where(col2 == lax.broadcasted_iota(jnp.int32, (c_, 2 * c_), 0) + c_, 1.0, 0.0)
    lane = lax.broadcasted_iota(jnp.int32, (nc, LANES), 1)

    rb = min(seq, PROLOGUE_ROWS)
    halo = CONV_K - 1

    def conv_silu(x_ref, cw_ref, dst, g, post):
        cols = slice(g * d, (g + 1) * d)
        w = cw_ref[:, cols]
        xpad[0:8, :] = jnp.zeros((8, d), F32)
        xpad[8:8 + rb, :] = x_ref[0:rb, cols]
        for r0 in range(0, seq, rb):
            acc = None
            for j in range(CONV_K):
                if r0 == 0:
                    xs = xpad[8 - halo + j:8 - halo + j + rb, :]
                else:
                    xs = x_ref[r0 - halo + j:r0 - halo + j + rb, cols]
                t = xs * w[j:j + 1, :]
                acc = t if acc is None else acc + t
            dst[g, r0:r0 + rb, :] = post(_silu(acc))

    l2n = lambda y: y * lax.rsqrt(jnp.sum(y * y, -1, keepdims=True) + 1e-6)

    for g in range(heads):
        conv_silu(q_ref, cwq_ref, qs, g, lambda y: l2n(y) * (d ** -0.5))
        conv_silu(k_ref, cwk_ref, ks, g, l2n)
        conv_silu(v_ref, cwv_ref, vs, g, lambda y: y)

        a = ga_ref[0, g] + dtb_ref[g]
        sp = jnp.maximum(a, 0.0) + jnp.log1p(jnp.exp(-jnp.abs(a)))
        gl = jnp.where(lane < c_, -jnp.exp(alog_ref[g]) * sp, 0.0)
        sh = 1
        while sh < c_:
            gl = gl + jnp.where(lane >= sh, pltpu.roll(gl, sh, 1), 0.0)
            sh *= 2
        gcs[g] = gl
        bts[g] = jax.nn.sigmoid(gb_ref[0, g])

    def phase1(cg, carry):
        chains = [(cg * unroll1 + j, g) for j in range(unroll1) for g in range(heads)]
        st = []
        for c, g in chains:
            r0 = pl.multiple_of(c * c_, c_)
            qc = qs[g, pl.ds(r0, c_), :]
            kc = ks[g, pl.ds(r0, c_), :]
            vc = vs[g, pl.ds(r0, c_), :]
            gcr = gcs[g, pl.ds(c, 1), :][:, :c_]
            btr = bts[g, pl.ds(c, 1), :][:, :c_]
            gcol = jnp.sum(jnp.where(eye, jnp.broadcast_to(gcr, (c_, c_)), 0.0), 1, keepdims=True)
            bcol = jnp.sum(jnp.where(eye, jnp.broadcast_to(btr, (c_, c_)), 0.0), 1, keepdims=True)
            glast = gcr[:, c_ - 1:c_]
            decay = jnp.where(tril, jnp.exp(jnp.where(tril, gcol - gcr, 0.0)), 0.0)
            kb = kc * bcol
            r = jnp.concatenate([vc * bcol, kb * jnp.exp(gcol)], axis=1)
            st.append(dict(c=c, g=g, r0=r0, qc=qc, kc=kc, kb=kb, gcol=gcol, glast=glast, decay=decay, r=r))
        for t in st:
            t["aq"] = _dot_nt(jnp.concatenate([t["kb"], t["qc"]], axis=0), t["kc"])
        for t in st:
            t["nl"] = jnp.where(strict, -(t["aq"][:c_] * t["decay"]), 0.0)
            t["intra"] = jnp.where(tril, t["aq"][c_:] * t["decay"], 0.0)
        n_levels = c_.bit_length() - 1
        for t in st:
            nt0 = jnp.concatenate([t["nl"], jnp.zeros((c_, c_), F32)], axis=1) + eye_r
            t["nt"] = _dot(t["nl"], nt0) + eye_r
        for lvl in range(1, n_levels):
            for t in st:
                nt = t["nt"]
                t["nt"] = _dot(nt[:, :c_], nt) + jnp.where(right, nt, 0.0)
        for t in st:
            t["r"] = _dot(t["nt"], jnp.concatenate([jnp.zeros_like(t["r"]), t["r"]], axis=0))
        for t in st:
            kd = t["kc"] * jnp.exp(t["glast"] - t["gcol"])
            t["kq"] = _dot_tn(kd, t["r"])
            t["io"] = _dot(t["intra"], t["r"])
        for t in st:
            c, g, r0 = t["c"], t["g"], t["r0"]
            rd = pl.multiple_of(c * d, d)
            q2s[g, pl.ds(rd, d), :] = t["kq"][:, :d]
            k2s[g, pl.ds(rd, d), :] = t["kq"][:, d:].astype(BF16)
            o0s[g, pl.ds(r0, c_), :] = t["io"][:, :d]
            rs[g, pl.ds(r0, c_), :] = (t["qc"] * jnp.exp(t["gcol"]) - t["io"][:, d:]).astype(BF16)
            egs[g, pl.ds(c, 1), :] = jnp.broadcast_to(jnp.exp(t["glast"]), (1, LANES))
        return carry

    lax.fori_loop(0, nc // unroll1, phase1, 0)

    def phase2(c, states):
        r0 = pl.multiple_of(c * c_, c_)
        rd = pl.multiple_of(c * d, d)
        sbs = [states[g].astype(BF16) for g in range(heads)]
        ks2 = [_dot(k2s[g, pl.ds(rd, d), :], sbs[g]) for g in range(heads)]
        ros = [_dot(rs[g, pl.ds(r0, c_), :], sbs[g]) for g in range(heads)]
        new_states = [states[g] * egs[g, pl.ds(c, 1), :] + q2s[g, pl.ds(rd, d), :] - ks2[g]
                      for g in range(heads)]
        for g in range(heads):
            o0s[g, pl.ds(r0, c_), :] += ros[g]
        return tuple(new_states)

    lax.fori_loop(0, nc, phase2, tuple(jnp.zeros((d, d), F32) for _ in range(heads)), unroll=unroll2)

    nw = nw_ref[...]
    for g in range(heads):
        cols = slice(g * d, (g + 1) * d)
        for r0 in range(0, seq, rb):
            o = o0s[g, r0:r0 + rb, :]
            on = o * lax.rsqrt(jnp.mean(o * o, -1, keepdims=True) + 1e-6) * nw
            o_ref[r0:r0 + rb, cols] = (on * _silu(z_ref[r0:r0 + rb, cols])).astype(o_ref.dtype)


def _gdn(proj, conv_w, ga_rows, gb_rows, a_log, dt_bias, norm_w, *, batch, seq, n_heads, heads):
    d = GDN_HEAD_DIM
    gw = heads * d
    hb = n_heads // heads
    nc = seq // CHUNK
    m = batch * seq
    kern = functools.partial(_gdn_kernel, heads=heads, seq=seq, unroll1=math.gcd(nc, 16), unroll2=2)
    col = lambda off: (lambda b, h: (b, off * hb + h))
    cw = lambda off: (lambda b, h: (0, off * hb + h))
    return pl.pallas_call(
        kern,
        grid=(batch, hb),
        in_specs=[
            pl.BlockSpec((seq, gw), col(0)), pl.BlockSpec((seq, gw), col(1)),
            pl.BlockSpec((seq, gw), col(2)), pl.BlockSpec((seq, gw), col(3)),
            pl.BlockSpec((CONV_K, gw), cw(0)), pl.BlockSpec((CONV_K, gw), cw(1)),
            pl.BlockSpec((CONV_K, gw), cw(2)),
            pl.BlockSpec((1, heads, nc, LANES), lambda b, h: (b, h, 0, 0)),
            pl.BlockSpec((1, heads, nc, LANES), lambda b, h: (b, h, 0, 0)),
            pl.BlockSpec((heads, 1, 1), lambda b, h: (h, 0, 0)),
            pl.BlockSpec((heads, 1, 1), lambda b, h: (h, 0, 0)),
            pl.BlockSpec((1, d), lambda b, h: (0, 0)),
        ],
        out_specs=pl.BlockSpec((seq, gw), lambda b, h: (b, h)),
        out_shape=jax.ShapeDtypeStruct((m, n_heads * d), BF16),
        scratch_shapes=[
            pltpu.VMEM((PROLOGUE_ROWS + 8, d), F32),
            pltpu.VMEM((heads, seq, d), F32),
            pltpu.VMEM((heads, seq, d), F32),
            pltpu.VMEM((heads, seq, d), F32),
            pltpu.VMEM((heads, nc * d, d), F32),
            pltpu.VMEM((heads, nc * d, d), BF16),
            pltpu.VMEM((heads, seq, d), F32),
            pltpu.VMEM((heads, seq, d), BF16),
            pltpu.VMEM((heads, nc, LANES), F32),
            pltpu.VMEM((heads, nc, LANES), F32),
            pltpu.VMEM((heads, nc, LANES), F32),
        ],
        compiler_params=pltpu.CompilerParams(
            dimension_semantics=("parallel", "parallel"), vmem_limit_bytes=VMEM_LIMIT),
        name="gdn",
    )(proj, proj, proj, proj, conv_w, conv_w, conv_w, ga_rows, gb_rows,
      a_log.reshape(n_heads, 1, 1), dt_bias.reshape(n_heads, 1, 1), norm_w.reshape(1, d))


def _diff_attn_kernel(q_ref, k_ref, vt_ref, lq1_ref, lk1_ref, lq2_ref, lk2_ref, nw_ref, sa_ref, sb_ref,
                      o_ref, sa_out_ref, sb_out_ref, acc1, acc2, *, tq, seq, lambda_init):
    dh = DIFF_HEAD_DIM
    c2 = (dh ** -0.5) * math.log2(math.e)
    neg = float(jnp.finfo(jnp.float32).min)
    krow = lax.broadcasted_iota(jnp.int32, (tq, tq), 0)
    qcol = lax.broadcasted_iota(jnp.int32, (tq, tq), 1)
    keep = krow <= qcol
    lam = (jnp.exp(jnp.sum(lq1_ref[...] * lk1_ref[...], -1, keepdims=True))
           - jnp.exp(jnp.sum(lq2_ref[...] * lk2_ref[...], -1, keepdims=True)) + lambda_init)
    nw = nw_ref[...] * (1.0 - lambda_init)

    nq = seq // tq
    side_cols = sa_ref.shape[1]
    piece = -(-side_cols // (nq * LANES)) * LANES
    for i in range(nq):
        for src, dst in ((sa_ref, sa_out_ref), (sb_ref, sb_out_ref)):
            if i * piece < side_cols:
                _cast_block(src, dst, i * piece, min((i + 1) * piece, side_cols))
        q1 = q_ref[i * tq:(i + 1) * tq, :dh]
        q2 = q_ref[i * tq:(i + 1) * tq, dh:]
        m1 = m2 = jnp.full((1, tq), neg, F32)
        l1 = l2 = jnp.zeros((1, tq), F32)
        for j in range(i + 1):
            kj = k_ref[j * tq:(j + 1) * tq, :]
            vt = vt_ref[:, j * tq:(j + 1) * tq]
            s1 = _dot_nt(kj[:, :dh], q1)
            s2 = _dot_nt(kj[:, dh:], q2)
            if j == i:
                s1 = jnp.where(keep, s1, neg)
                s2 = jnp.where(keep, s2, neg)
            m1n = jnp.maximum(m1, jnp.max(s1, 0, keepdims=True))
            m2n = jnp.maximum(m2, jnp.max(s2, 0, keepdims=True))
            a1 = jnp.exp2((m1 - m1n) * c2)
            a2 = jnp.exp2((m2 - m2n) * c2)
            p1 = jnp.exp2((s1 - m1n) * c2)
            p2 = jnp.exp2((s2 - m2n) * c2)
            l1 = a1 * l1 + jnp.sum(p1, 0, keepdims=True)
            l2 = a2 * l2 + jnp.sum(p2, 0, keepdims=True)
            pv1 = _dot(vt, p1)
            pv2 = _dot(vt, p2)
            if j == 0:
                acc1[...] = pv1
                acc2[...] = pv2
            else:
                acc1[...] = acc1[...] * a1 + pv1
                acc2[...] = acc2[...] * a2 + pv2
            m1, m2 = m1n, m2n
        o = acc1[...] / l1 - lam * (acc2[...] / l2)
        on = o * lax.rsqrt(jnp.mean(o * o, 0, keepdims=True) + 1e-6) * nw
        o_ref[i * tq:(i + 1) * tq, :] = on.T.astype(o_ref.dtype)


def _diff_attn(qk, vt, lq1, lk1, lq2, lk2, norm_w, side_a, side_b, *, batch, seq, n_heads, tq, lambda_init):
    dv = DIFF_V_DIM
    m = batch * seq
    steps = batch * n_heads
    assert side_a.shape == side_b.shape and side_a.shape[0] % (16 * steps) == 0
    srows, scols = side_a.shape[0] // steps, side_a.shape[1]
    slab = pl.BlockSpec((srows, scols), lambda b, h: (b * n_heads + h, 0))
    kern = functools.partial(_diff_attn_kernel, tq=tq, seq=seq, lambda_init=lambda_init)
    vec = lambda n: pl.BlockSpec((1, n), lambda b, h: (0, 0))
    return pl.pallas_call(
        kern,
        grid=(batch, n_heads),
        in_specs=[
            pl.BlockSpec((seq, dv), lambda b, h: (b, h)),
            pl.BlockSpec((seq, dv), lambda b, h: (b, n_heads + h)),
            pl.BlockSpec((dv, seq), lambda b, h: (h, b)),
            vec(DIFF_HEAD_DIM), vec(DIFF_HEAD_DIM), vec(DIFF_HEAD_DIM), vec(DIFF_HEAD_DIM),
            pl.BlockSpec((dv, 1), lambda b, h: (0, 0)),
            slab, slab,
        ],
        out_specs=[pl.BlockSpec((seq, dv), lambda b, h: (b, h)), slab, slab],
        out_shape=[jax.ShapeDtypeStruct((m, n_heads * dv), BF16),
                   jax.ShapeDtypeStruct(side_a.shape, BF16), jax.ShapeDtypeStruct(side_b.shape, BF16)],
        scratch_shapes=[pltpu.VMEM((dv, tq), F32), pltpu.VMEM((dv, tq), F32)],
        compiler_params=pltpu.CompilerParams(
            dimension_semantics=("parallel", "parallel"), vmem_limit_bytes=VMEM_LIMIT),
        name="diff_attn",
    )(qk, qk, vt, lq1.reshape(1, -1), lk1.reshape(1, -1), lq2.reshape(1, -1),
      lk2.reshape(1, -1), norm_w.reshape(-1, 1), side_a, side_b)


def _layer_norm(y, g, b):
    mu = jnp.mean(y, -1, keepdims=True)
    yc = y - mu
    var = jnp.mean(yc * yc, -1, keepdims=True)
    return yc * lax.rsqrt(var + 1e-5) * g + b


LN_ROWS = 64


def _deepnorm_epilogue(x_ref, acc_ref, g_ref, beta_ref, alpha, out_refs):
    g = g_ref[...]
    beta = beta_ref[...]

    def rows(i, carry):
        r0 = pl.multiple_of(i * LN_ROWS, LN_ROWS)
        sl = pl.ds(r0, LN_ROWS)
        y = _layer_norm(alpha * x_ref[sl, :] + acc_ref[sl, :], g, beta)
        for o in out_refs:
            o[sl, :] = y.astype(o.dtype)
        return carry

    lax.fori_loop(0, x_ref.shape[0] // LN_ROWS, rows, 0)


class _DeepNormStream:
    def __init__(self, x_hbm, out_hbms, row0, acc_ref, xin, youts, sem_in, sem_out):
        self.x_hbm, self.out_hbms, self.row0, self.acc_ref = x_hbm, out_hbms, row0, acc_ref
        self.xin, self.youts, self.sem_in, self.sem_out = xin, youts, sem_in, sem_out
        self.n = acc_ref.shape[0] // LN_ROWS
        assert acc_ref.shape[0] % LN_ROWS == 0 and self.n >= 2

    def _rows(self, c):
        return pl.ds(pl.multiple_of(self.row0 + c * LN_ROWS, LN_ROWS), LN_ROWS)

    def in_copy(self, c, slot):
        return pltpu.make_async_copy(self.x_hbm.at[self._rows(c), :], self.xin.at[slot], self.sem_in.at[slot])

    def out_copy(self, k, c, slot):
        return pltpu.make_async_copy(self.youts[k].at[slot], self.out_hbms[k].at[self._rows(c), :],
                                     self.sem_out.at[k, slot])

    def prefetch(self):
        self.in_copy(0, 0).start()

    def run(self, g_ref, beta_ref, alpha):
        g = g_ref[...]
        beta = beta_ref[...]
        n, nout = self.n, len(self.out_hbms)

        def chunk(c, carry):
            slot = c % 2
            self.in_copy(c, slot).wait()

            @pl.when(c + 1 < n)
            def _():
                self.in_copy(c + 1, 1 - slot).start()

            @pl.when(c >= 2)
            def _():
                for k in range(nout):
                    self.out_copy(k, c - 2, slot).wait()

            r0 = pl.multiple_of(c * LN_ROWS, LN_ROWS)
            y = _layer_norm(alpha * self.xin[slot] + self.acc_ref[pl.ds(r0, LN_ROWS), :], g, beta)
            for k in range(nout):
                self.youts[k][slot] = y.astype(self.youts[k].dtype)
                self.out_copy(k, c, slot).start()
            return carry

        lax.fori_loop(0, n, chunk, 0)
        for c in (n - 2, n - 1):
            for k in range(nout):
                self.out_copy(k, c, c % 2).wait()


def _out_proj_kernel(a_ref, b_ref, w_ref, g_ref, beta_ref, x_hbm, o_hbm, ob_hbm,
                     acc, xin, yout, youtb, sem_in, sem_out, *, nk, alpha):
    i = pl.program_id(0)
    k = pl.program_id(1)
    half = nk // 2
    bm, n = acc.shape
    stream = _DeepNormStream(x_hbm, (o_hbm, ob_hbm), i * bm, acc, xin, (yout, youtb), sem_in, sem_out)

    @pl.when(k == nk - 1)
    def _():
        stream.prefetch()

    def accumulate(lhs_ref, first):
        nb = min(n, 512)
        lhs = lhs_ref[...]
        for n0 in range(0, n, nb):
            part = jnp.dot(lhs, w_ref[:, n0:n0 + nb], preferred_element_type=F32)
            if first:
                acc[:, n0:n0 + nb] = part
            else:
                acc[:, n0:n0 + nb] += part

    @pl.when(k == 0)
    def _():
        accumulate(a_ref, True)

    @pl.when(jnp.logical_and(k > 0, k < half))
    def _():
        accumulate(a_ref, False)

    @pl.when(k >= half)
    def _():
        accumulate(b_ref, False)

    @pl.when(k == nk - 1)
    def _():
        stream.run(g_ref, beta_ref, alpha)


def _out_proj_ln(a, b, w, x, g, beta, *, bm, bk, alpha):
    m, ka = a.shape
    n = w.shape[1]
    half = ka // bk
    nk = 2 * half
    assert ka % bk == 0 and m % bm == 0 and half >= 1
    kern = functools.partial(_out_proj_kernel, nk=nk, alpha=alpha)
    return pl.pallas_call(
        kern,
        grid=(m // bm, nk),
        in_specs=[
            pl.BlockSpec((bm, bk), lambda i, k: (i, jnp.minimum(k, half - 1))),
            pl.BlockSpec((bm, bk), lambda i, k: (i, jnp.maximum(k - half, 0))),
            pl.BlockSpec((bk, n), lambda i, k: (k, 0)),
            pl.BlockSpec((1, n), lambda i, k: (0, 0)),
            pl.BlockSpec((1, n), lambda i, k: (0, 0)),
            pl.BlockSpec(memory_space=pl.ANY),
        ],
        out_specs=[pl.BlockSpec(memory_space=pl.ANY), pl.BlockSpec(memory_space=pl.ANY)],
        out_shape=[jax.ShapeDtypeStruct((m, n), F32), jax.ShapeDtypeStruct((m, n), BF16)],
        scratch_shapes=[
            pltpu.VMEM((bm, n), F32),
            pltpu.VMEM((2, LN_ROWS, n), F32),
            pltpu.VMEM((2, LN_ROWS, n), F32),
            pltpu.VMEM((2, LN_ROWS, n), BF16),
            pltpu.SemaphoreType.DMA((2,)),
            pltpu.SemaphoreType.DMA((2, 2)),
        ],
        compiler_params=pltpu.CompilerParams(
            dimension_semantics=("arbitrary", "arbitrary"), vmem_limit_bytes=VMEM_LIMIT),
        name="out_proj_ln",
    )(a, b, w, g.reshape(1, n), beta.reshape(1, n), x)


def _ffn_kernel(xb_ref, wg_ref, wu_ref, wd_ref, x_ref, g_ref, beta_ref, o_ref, *, nf, alpha):
    f = pl.program_id(1)

    @pl.when(f == 0)
    def _():
        o_ref[...] = jnp.zeros_like(o_ref)

    xb = xb_ref[...]
    gate = jnp.dot(xb, wg_ref[...], preferred_element_type=F32)
    up = jnp.dot(xb, wu_ref[...], preferred_element_type=F32)
    hid = (_silu(gate) * up).astype(BF16)
    dm = o_ref.shape[1]
    nb = min(dm, 512)
    for n0 in range(0, dm, nb):
        o_ref[:, n0:n0 + nb] += jnp.dot(hid, wd_ref[:, n0:n0 + nb], preferred_element_type=F32)

    @pl.when(f == nf - 1)
    def _():
        _deepnorm_epilogue(x_ref, o_ref, g_ref, beta_ref, alpha, (o_ref,))


def _ffn_ln(xb, x, wg, wu, wd, g, beta, *, bm, bf, alpha):
    m, dm = xb.shape
    ff = wg.shape[1]
    nf = ff // bf
    assert ff % bf == 0
    kern = functools.partial(_ffn_kernel, nf=nf, alpha=alpha)
    return pl.pallas_call(
        kern,
        grid=(m // bm, nf),
        in_specs=[
            pl.BlockSpec((bm, dm), lambda i, f: (i, 0)),
            pl.BlockSpec((dm, bf), lambda i, f: (0, f)),
            pl.BlockSpec((dm, bf), lambda i, f: (0, f)),
            pl.BlockSpec((bf, dm), lambda i, f: (f, 0)),
            pl.BlockSpec((bm, dm), lambda i, f: (i, 0)),
            pl.BlockSpec((1, dm), lambda i, f: (0, 0)),
            pl.BlockSpec((1, dm), lambda i, f: (0, 0)),
        ],
        out_specs=pl.BlockSpec((bm, dm), lambda i, f: (i, 0)),
        out_shape=jax.ShapeDtypeStruct((m, dm), F32),
        compiler_params=pltpu.CompilerParams(
            dimension_semantics=("parallel", "arbitrary"), vmem_limit_bytes=VMEM_LIMIT),
        name="ffn_ln",
    )(xb, wg, wu, wd, x, g.reshape(1, dm), beta.reshape(1, dm))


def _layer(x2, batch, seq, l, w_in, conv_w, a_log, dt_bias, gdn_norm_w, lq1, lk1, lq2, lk2,
           diff_norm_w, w_out, ln1_g, ln1_b, w_gate, w_up, w_down, ln2_g, ln2_b, alpha):
    m, dm = x2.shape
    n_gdn_heads = a_log.shape[0]
    gdn_w = n_gdn_heads * GDN_HEAD_DIM
    diff_w = w_out.shape[0] - gdn_w
    n_diff_heads = diff_w // DIFF_V_DIM
    lambda_init = 0.8 - 0.6 * math.exp(-0.3 * l)
    nc = seq // CHUNK

    bn = 768
    main = 4 * gdn_w
    ab = 2 * n_gdn_heads
    wide = -(-(main + ab) // bn) * bn
    w_t = w_in.T
    xb = x2.astype(BF16)

    proj_g = _matmul_wt32(xb, w_t, 0, wide, 1024, bn, F32, "proj_gdn")
    qk_w = 2 * n_diff_heads * 2 * DIFF_HEAD_DIM
    proj_qk, wd_b = _matmul_wt32(xb, w_t, main + ab, qk_w, 1024, 512, BF16, "proj_qk",
                                 side=(w_down, 256))
    proj_vt, wo_b = _matmul_wt32(xb, w_t, main + ab + qk_w, diff_w, 1024, 512, BF16, "proj_vt",
                                 transposed_out=True, side=(w_out, 128))

    gates = proj_g[:, main:main + ab].reshape(batch, seq, ab).transpose(0, 2, 1)
    gates = gates.reshape(batch, ab, nc, CHUNK)
    gates = jnp.pad(gates, ((0, 0), (0, 0), (0, 0), (0, LANES - CHUNK)))
    gdn_out = _gdn(proj_g, conv_w, gates[:, :n_gdn_heads], gates[:, n_gdn_heads:], a_log, dt_bias,
                   gdn_norm_w, batch=batch, seq=seq, n_heads=n_gdn_heads, heads=2)

    diff_out, wg_b, wu_b = _diff_attn(proj_qk, proj_vt, lq1, lk1, lq2, lk2, diff_norm_w, w_gate, w_up,
                                      batch=batch, seq=seq, n_heads=n_diff_heads, tq=512,
                                      lambda_init=lambda_init)

    x1, x1b = _out_proj_ln(gdn_out, diff_out, wo_b, x2, ln1_g, ln1_b, bm=1024, bk=1024, alpha=alpha)
    return _ffn_ln(x1b, x1, wg_b, wu_b, wd_b, ln2_g, ln2_b, bm=512, bf=256, alpha=alpha)


def kernel(x, w_in, conv_w, a_log, dt_bias, gdn_norm_w, lambda_q1, lambda_k1, lambda_q2, lambda_k2,
           diff_norm_w, w_out, ln1_g, ln1_b, w_gate, w_up, w_down, ln2_g, ln2_b):
    batch, seq, dm = x.shape
    depth = w_in.shape[0]
    alpha = (2 * depth) ** 0.25
    x2 = x.reshape(batch * seq, dm)
    for l in range(depth):
        x2 = _layer(x2, batch, seq, l, w_in[l], conv_w[l], a_log[l], dt_bias[l], gdn_norm_w[l],
                    lambda_q1[l], lambda_k1[l], lambda_q2[l], lambda_k2[l], diff_norm_w[l], w_out[l],
                    ln1_g[l], ln1_b[l], w_gate[l], w_up[l], w_down[l], ln2_g[l], ln2_b[l], alpha)
    return x2.reshape(batch, seq, dm)
```

```python
import functools
import math

import jax
import jax.numpy as jnp
from jax import lax
from jax.experimental import pallas as pl
from jax.experimental.pallas import tpu as pltpu

F32 = jnp.float32
BF16 = jnp.bfloat16

GDN_HEAD_DIM = 128
DIFF_HEAD_DIM = 128
DIFF_V_DIM = 256
CONV_K = 4
CHUNK = 64
PROLOGUE_ROWS = 128
LANES = 128
VMEM_LIMIT = 60 * 1024 * 1024


def _dot(a, b):
    return jnp.dot(a.astype(BF16), b.astype(BF16), preferred_element_type=F32)


def _dot_nt(a, b):
    return lax.dot_general(a.astype(BF16), b.astype(BF16), (((1,), (1,)), ((), ())),
                           preferred_element_type=F32)


def _dot_tn(a, b):
    return lax.dot_general(a.astype(BF16), b.astype(BF16), (((0,), (0,)), ((), ())),
                           preferred_element_type=F32)


def _silu(x):
    return x * jax.nn.sigmoid(x)


CAST_ROWS = 128


SIDE_COLS = 1024


def _cast_block(src_ref, dst_ref, col0=0, col1=None):
    col1 = src_ref.shape[1] if col1 is None else col1
    for c0 in range(col0, col1, SIDE_COLS):
        c1 = min(c0 + SIDE_COLS, col1)
        dst_ref[:, c0:c1] = src_ref[:, c0:c1].astype(BF16)


def _mm_wt32_kernel(x_ref, w_ref, *rest, transposed_out, has_side):
    if has_side:
        side_ref, o_ref, side_out_ref, wb_ref = rest
        _cast_block(side_ref, side_out_ref)
    else:
        o_ref, wb_ref = rest

    @pl.when(pl.program_id(1) == 0)
    def _():
        for r in range(0, w_ref.shape[0], CAST_ROWS):
            wb_ref[r:r + CAST_ROWS, :] = w_ref[r:r + CAST_ROWS, :].astype(BF16)

    if transposed_out:
        o_ref[...] = _dot_nt(wb_ref[...], x_ref[...]).astype(o_ref.dtype)
    else:
        o_ref[...] = _dot_nt(x_ref[...], wb_ref[...]).astype(o_ref.dtype)


def _matmul_wt32(x, w_t, row0, n_out, bm, bn, out_dtype, name, transposed_out=False, side=None):
    m, k = x.shape
    assert m % bm == 0 and n_out % bn == 0 and bn % CAST_ROWS == 0 and row0 % 8 == 0
    assert row0 + n_out <= w_t.shape[0] and k == w_t.shape[1]
    nj, ni = n_out // bn, m // bm
    if transposed_out:
        out_specs = [pl.BlockSpec((bn, bm), lambda j, i: (j, i))]
        out_shape = [jax.ShapeDtypeStruct((n_out, m), out_dtype)]
    else:
        out_specs = [pl.BlockSpec((bm, bn), lambda j, i: (i, j))]
        out_shape = [jax.ShapeDtypeStruct((m, n_out), out_dtype)]
    in_specs = [pl.BlockSpec((bm, k), lambda j, i: (i, 0)),
                pl.BlockSpec((pl.Element(bn), pl.Element(k)), lambda j, i: ((row0 // 8 + j * (bn // 8)) * 8, 0))]
    args = [x, w_t]
    if side is not None:
        a, rows = side
        n_slabs = a.shape[0] // rows
        assert a.shape[0] % rows == 0 and rows % 16 == 0 and n_slabs <= nj * ni
        slab = lambda j, i: (jnp.minimum(j * ni + i, n_slabs - 1), 0)
        in_specs.append(pl.BlockSpec((rows, a.shape[1]), slab))
        out_specs.append(pl.BlockSpec((rows, a.shape[1]), slab))
        out_shape.append(jax.ShapeDtypeStruct(a.shape, BF16))
        args.append(a)
    outs = pl.pallas_call(
        functools.partial(_mm_wt32_kernel, transposed_out=transposed_out, has_side=side is not None),
        grid=(nj, ni),
        in_specs=in_specs,
        out_specs=out_specs,
        out_shape=out_shape,
        scratch_shapes=[pltpu.VMEM((bn, k), BF16)],
        compiler_params=pltpu.CompilerParams(
            dimension_semantics=("arbitrary", "arbitrary"), vmem_limit_bytes=VMEM_LIMIT),
        name=name,
    )(*args)
    return outs if side is not None else outs[0]


def _gdn_kernel(q_ref, k_ref, v_ref, z_ref, cwq_ref, cwk_ref, cwv_ref, ga_ref, gb_ref,
                alog_ref, dtb_ref, nw_ref, o_ref,
                xpad, qs, ks, vs, q2s, k2s, o0s, rs, gcs, bts, egs, *, heads, seq, unroll1, unroll2):
    d = GDN_HEAD_DIM
    c_ = CHUNK
    nc = seq // c_
    row = lax.broadcasted_iota(jnp.int32, (c_, c_), 0)
    col = lax.broadcasted_iota(jnp.int32, (c_, c_), 1)
    tril = col <= row
    strict = col < row
    eye = col == row
    col2 = lax.broadcasted_iota(jnp.int32, (c_, 2 * c_), 1)
    right = col2 >= c_
    eye_r = jnp.where(col2 == lax.broadcasted_iota(jnp.int32, (c_, 2 * c_), 0) + c_, 1.0, 0.0)
    lane = lax.broadcasted_iota(jnp.int32, (nc, LANES), 1)

    rb = min(seq, PROLOGUE_ROWS)
    halo = CONV_K - 1

    def conv_silu(x_ref, cw_ref, dst, g, post):
        cols = slice(g * d, (g + 1) * d)
        w = cw_ref[:, cols]
        xpad[0:8, :] = jnp.zeros((8, d), F32)
        xpad[8:8 + rb, :] = x_ref[0:rb, cols]
        for r0 in range(0, seq, rb):
            acc = None
            for j in range(CONV_K):
                if r0 == 0:
                    xs = xpad[8 - halo + j:8 - halo + j + rb, :]
                else:
                    xs = x_ref[r0 - halo + j:r0 - halo + j + rb, cols]
                t = xs * w[j:j + 1, :]
                acc = t if acc is None else acc + t
            dst[g, r0:r0 + rb, :] = post(_silu(acc))

    l2n = lambda y: y * lax.rsqrt(jnp.sum(y * y, -1, keepdims=True) + 1e-6)

    for g in range(heads):
        conv_silu(q_ref, cwq_ref, qs, g, lambda y: l2n(y) * (d ** -0.5))
        conv_silu(k_ref, cwk_ref, ks, g, l2n)
        conv_silu(v_ref, cwv_ref, vs, g, lambda y: y)

        a = ga_ref[0, g] + dtb_ref[g]
        sp = jnp.maximum(a, 0.0) + jnp.log1p(jnp.exp(-jnp.abs(a)))
        gl = jnp.where(lane < c_, -jnp.exp(alog_ref[g]) * sp, 0.0)
        sh = 1
        while sh < c_:
            gl = gl + jnp.where(lane >= sh, pltpu.roll(gl, sh, 1), 0.0)
            sh *= 2
        gcs[g] = gl
        bts[g] = jax.nn.sigmoid(gb_ref[0, g])

    def phase1(cg, between_levels):
        chains = [(cg * unroll1 + j, g) for j in range(unroll1) for g in range(heads)]
        st = []
        for c, g in chains:
            r0 = pl.multiple_of(c * c_, c_)
            qc = qs[g, pl.ds(r0, c_), :]
            kc = ks[g, pl.ds(r0, c_), :]
            vc = vs[g, pl.ds(r0, c_), :]
            gcr = gcs[g, pl.ds(c, 1), :][:, :c_]
            btr = bts[g, pl.ds(c, 1), :][:, :c_]
            gcol = jnp.sum(jnp.where(eye, jnp.broadcast_to(gcr, (c_, c_)), 0.0), 1, keepdims=True)
            bcol = jnp.sum(jnp.where(eye, jnp.broadcast_to(btr, (c_, c_)), 0.0), 1, keepdims=True)
            glast = gcr[:, c_ - 1:c_]
            decay = jnp.where(tril, jnp.exp(jnp.where(tril, gcol - gcr, 0.0)), 0.0)
            kb = kc * bcol
            r = jnp.concatenate([vc * bcol, kb * jnp.exp(gcol)], axis=1)
            st.append(dict(c=c, g=g, r0=r0, qc=qc, kc=kc, kb=kb, gcol=gcol, glast=glast, decay=decay, r=r))
        for t in st:
            t["aq"] = _dot_nt(jnp.concatenate([t["kb"], t["qc"]], axis=0), t["kc"])
        between_levels()
        for t in st:
            t["nl"] = jnp.where(strict, -(t["aq"][:c_] * t["decay"]), 0.0)
            t["intra"] = jnp.where(tril, t["aq"][c_:] * t["decay"], 0.0)
        n_levels = c_.bit_length() - 1
        for t in st:
            nt0 = jnp.concatenate([t["nl"], jnp.zeros((c_, c_), F32)], axis=1) + eye_r
            t["nt"] = _dot(t["nl"], nt0) + eye_r
        between_levels()
        for lvl in range(1, n_levels):
            for t in st:
                nt = t["nt"]
                t["nt"] = _dot(nt[:, :c_], nt) + jnp.where(right, nt, 0.0)
            between_levels()
        for t in st:
            t["r"] = _dot(t["nt"], jnp.concatenate([jnp.zeros_like(t["r"]), t["r"]], axis=0))
        between_levels()
        for t in st:
            kd = t["kc"] * jnp.exp(t["glast"] - t["gcol"])
            t["kq"] = _dot_tn(kd, t["r"])
            t["io"] = _dot(t["intra"], t["r"])
        between_levels()
        for t in st:
            c, g, r0 = t["c"], t["g"], t["r0"]
            rd = pl.multiple_of(c * d, d)
            q2s[g, pl.ds(rd, d), :] = t["kq"][:, :d]
            k2s[g, pl.ds(rd, d), :] = t["kq"][:, d:].astype(BF16)
            o0s[g, pl.ds(r0, c_), :] = t["io"][:, :d]
            rs[g, pl.ds(r0, c_), :] = (t["qc"] * jnp.exp(t["gcol"]) - t["io"][:, d:]).astype(BF16)
            egs[g, pl.ds(c, 1), :] = jnp.broadcast_to(jnp.exp(t["glast"]), (1, LANES))

    def phase2(c, states):
        r0 = pl.multiple_of(c * c_, c_)
        rd = pl.multiple_of(c * d, d)
        sbs = [states[g].astype(BF16) for g in range(heads)]
        ks2 = [_dot(k2s[g, pl.ds(rd, d), :], sbs[g]) for g in range(heads)]
        ros = [_dot(rs[g, pl.ds(r0, c_), :], sbs[g]) for g in range(heads)]
        new_states = [states[g] * egs[g, pl.ds(c, 1), :] + q2s[g, pl.ds(rd, d), :] - ks2[g]
                      for g in range(heads)]
        for g in range(heads):
            o0s[g, pl.ds(r0, c_), :] += ros[g]
        return tuple(new_states)

    n_blocks = nc // unroll1
    n_hooks = c_.bit_length() + 3
    per_hook = -(-unroll1 // n_hooks)

    def block0(_, carry):
        phase1(0, lambda: None)
        return carry

    lax.fori_loop(0, 1, block0, 0)

    def pipelined(blk, states):
        box, done = [states], [0]

        def recurrence_steps():
            for _ in range(per_hook):
                if done[0] < unroll1:
                    box[0] = phase2((blk - 1) * unroll1 + done[0], box[0])
                    done[0] += 1

        phase1(blk, recurrence_steps)
        while done[0] < unroll1:
            recurrence_steps()
        return box[0]

    states = lax.fori_loop(1, n_blocks, pipelined, tuple(jnp.zeros((d, d), F32) for _ in range(heads)))
    lax.fori_loop((n_blocks - 1) * unroll1, nc, phase2, states, unroll=unroll2)

    nw = nw_ref[...]
    for g in range(heads):
        cols = slice(g * d, (g + 1) * d)
        for r0 in range(0, seq, rb):
            o = o0s[g, r0:r0 + rb, :]
            on = o * lax.rsqrt(jnp.mean(o * o, -1, keepdims=True) + 1e-6) * nw
            o_ref[r0:r0 + rb, cols] = (on * _silu(z_ref[r0:r0 + rb, cols])).astype(o_ref.dtype)


def _gdn(proj, conv_w, ga_rows, gb_rows, a_log, dt_bias, norm_w, *, batch, seq, n_heads, heads):
    d = GDN_HEAD_DIM
    gw = heads * d
    hb = n_heads // heads
    nc = seq // CHUNK
    m = batch * seq
    kern = functools.partial(_gdn_kernel, heads=heads, seq=seq, unroll1=math.gcd(nc, 16), unroll2=2)
    col = lambda off: (lambda b, h: (b, off * hb + h))
    cw = lambda off: (lambda b, h: (0, off * hb + h))
    return pl.pallas_call(
        kern,
        grid=(batch, hb),
        in_specs=[
            pl.BlockSpec((seq, gw), col(0)), pl.BlockSpec((seq, gw), col(1)),
            pl.BlockSpec((seq, gw), col(2)), pl.BlockSpec((seq, gw), col(3)),
            pl.BlockSpec((CONV_K, gw), cw(0)), pl.BlockSpec((CONV_K, gw), cw(1)),
            pl.BlockSpec((CONV_K, gw), cw(2)),
            pl.BlockSpec((1, heads, nc, LANES), lambda b, h: (b, h, 0, 0)),
            pl.BlockSpec((1, heads, nc, LANES), lambda b, h: (b, h, 0, 0)),
            pl.BlockSpec((heads, 1, 1), lambda b, h: (h, 0, 0)),
            pl.BlockSpec((heads, 1, 1), lambda b, h: (h, 0, 0)),
            pl.BlockSpec((1, d), lambda b, h: (0, 0)),
        ],
        out_specs=pl.BlockSpec((seq, gw), lambda b, h: (b, h)),
        out_shape=jax.ShapeDtypeStruct((m, n_heads * d), BF16),
        scratch_shapes=[
            pltpu.VMEM((PROLOGUE_ROWS + 8, d), F32),
            pltpu.VMEM((heads, seq, d), F32),
            pltpu.VMEM((heads, seq, d), F32),
            pltpu.VMEM((heads, seq, d), F32),
            pltpu.VMEM((heads, nc * d, d), F32),
            pltpu.VMEM((heads, nc * d, d), BF16),
            pltpu.VMEM((heads, seq, d), F32),
            pltpu.VMEM((heads, seq, d), BF16),
            pltpu.VMEM((heads, nc, LANES), F32),
            pltpu.VMEM((heads, nc, LANES), F32),
            pltpu.VMEM((heads, nc, LANES), F32),
        ],
        compiler_params=pltpu.CompilerParams(
            dimension_semantics=("parallel", "parallel"), vmem_limit_bytes=VMEM_LIMIT),
        name="gdn",
    )(proj, proj, proj, proj, conv_w, conv_w, conv_w, ga_rows, gb_rows,
      a_log.reshape(n_heads, 1, 1), dt_bias.reshape(n_heads, 1, 1), norm_w.reshape(1, d))


def _diff_attn_kernel(q_ref, k_ref, vt_ref, lq1_ref, lk1_ref, lq2_ref, lk2_ref, nw_ref, sa_ref, sb_ref,
                      o_ref, sa_out_ref, sb_out_ref, acc1, acc2, *, tq, tk, seq, lambda_init):
    dh = DIFF_HEAD_DIM
    c2 = (dh ** -0.5) * math.log2(math.e)
    neg = float(jnp.finfo(jnp.float32).min)
    krow = lax.broadcasted_iota(jnp.int32, (tk, tq), 0)
    qcol = lax.broadcasted_iota(jnp.int32, (tk, tq), 1)
    lam = (jnp.exp(jnp.sum(lq1_ref[...] * lk1_ref[...], -1, keepdims=True))
           - jnp.exp(jnp.sum(lq2_ref[...] * lk2_ref[...], -1, keepdims=True)) + lambda_init)
    nw = nw_ref[...] * (1.0 - lambda_init)

    nq = seq // tq
    side_cols = sa_ref.shape[1]
    piece = -(-side_cols // (nq * LANES)) * LANES
    for i in range(nq):
        for src, dst in ((sa_ref, sa_out_ref), (sb_ref, sb_out_ref)):
            if i * piece < side_cols:
                _cast_block(src, dst, i * piece, min((i + 1) * piece, side_cols))
        q1 = q_ref[i * tq:(i + 1) * tq, :dh]
        q2 = q_ref[i * tq:(i + 1) * tq, dh:]
        m1 = m2 = jnp.full((1, tq), neg, F32)
        l1 = l2 = jnp.zeros((1, tq), F32)
        for j in range((i + 1) * tq // tk):
            kj = k_ref[j * tk:(j + 1) * tk, :]
            vt = vt_ref[:, j * tk:(j + 1) * tk]
            s1 = _dot_nt(kj[:, :dh], q1)
            s2 = _dot_nt(kj[:, dh:], q2)
            if (j + 1) * tk - 1 > i * tq:
                keep = krow + j * tk <= qcol + i * tq
                s1 = jnp.where(keep, s1, neg)
                s2 = jnp.where(keep, s2, neg)
            m1n = jnp.maximum(m1, jnp.max(s1, 0, keepdims=True))
            a1 = jnp.exp2((m1 - m1n) * c2)
            p1 = jnp.exp2((s1 - m1n) * c2)
            l1 = a1 * l1 + jnp.sum(p1, 0, keepdims=True)
            pv1 = _dot(vt, p1)
            m2n = jnp.maximum(m2, jnp.max(s2, 0, keepdims=True))
            a2 = jnp.exp2((m2 - m2n) * c2)
            p2 = jnp.exp2((s2 - m2n) * c2)
            l2 = a2 * l2 + jnp.sum(p2, 0, keepdims=True)
            pv2 = _dot(vt, p2)
            if j == 0:
                acc1[...] = pv1
                acc2[...] = pv2
            else:
                acc1[...] = acc1[...] * a1 + pv1
                acc2[...] = acc2[...] * a2 + pv2
            m1, m2 = m1n, m2n
        o = acc1[...] / l1 - lam * (acc2[...] / l2)
        on = o * lax.rsqrt(jnp.mean(o * o, 0, keepdims=True) + 1e-6) * nw
        o_ref[i * tq:(i + 1) * tq, :] = on.T.astype(o_ref.dtype)


def _diff_attn(qk, vt, lq1, lk1, lq2, lk2, norm_w, side_a, side_b, *, batch, seq, n_heads, tq, tk, lambda_init):
    dv = DIFF_V_DIM
    m = batch * seq
    steps = batch * n_heads
    assert side_a.shape == side_b.shape and side_a.shape[0] % (16 * steps) == 0
    srows, scols = side_a.shape[0] // steps, side_a.shape[1]
    slab = pl.BlockSpec((srows, scols), lambda b, h: (b * n_heads + h, 0))
    assert seq % tq == 0 and tq % tk == 0
    kern = functools.partial(_diff_attn_kernel, tq=tq, tk=tk, seq=seq, lambda_init=lambda_init)
    vec = lambda n: pl.BlockSpec((1, n), lambda b, h: (0, 0))
    return pl.pallas_call(
        kern,
        grid=(batch, n_heads),
        in_specs=[
            pl.BlockSpec((seq, dv), lambda b, h: (b, h)),
            pl.BlockSpec((seq, dv), lambda b, h: (b, n_heads + h)),
            pl.BlockSpec((dv, seq), lambda b, h: (h, b)),
            vec(DIFF_HEAD_DIM), vec(DIFF_HEAD_DIM), vec(DIFF_HEAD_DIM), vec(DIFF_HEAD_DIM),
            pl.BlockSpec((dv, 1), lambda b, h: (0, 0)),
            slab, slab,
        ],
        out_specs=[pl.BlockSpec((seq, dv), lambda b, h: (b, h)), slab, slab],
        out_shape=[jax.ShapeDtypeStruct((m, n_heads * dv), BF16),
                   jax.ShapeDtypeStruct(side_a.shape, BF16), jax.ShapeDtypeStruct(side_b.shape, BF16)],
        scratch_shapes=[pltpu.VMEM((dv, tq), F32), pltpu.VMEM((dv, tq), F32)],
        compiler_params=pltpu.CompilerParams(
            dimension_semantics=("parallel", "parallel"), vmem_limit_bytes=VMEM_LIMIT),
        name="diff_attn",
    )(qk, qk, vt, lq1.reshape(1, -1), lk1.reshape(1, -1), lq2.reshape(1, -1),
      lk2.reshape(1, -1), norm_w.reshape(-1, 1), side_a, side_b)


def _layer_norm(y, g, b):
    mu = jnp.mean(y, -1, keepdims=True)
    yc = y - mu
    var = jnp.mean(yc * yc, -1, keepdims=True)
    return yc * lax.rsqrt(var + 1e-5) * g + b


LN_ROWS = 64


def _deepnorm_epilogue(x_ref, acc_ref, g_ref, beta_ref, alpha, out_refs):
    g = g_ref[...]
    beta = beta_ref[...]

    def rows(i, carry):
        r0 = pl.multiple_of(i * LN_ROWS, LN_ROWS)
        sl = pl.ds(r0, LN_ROWS)
        y = _layer_norm(alpha * x_ref[sl, :] + acc_ref[sl, :], g, beta)
        for o in out_refs:
            o[sl, :] = y.astype(o.dtype)
        return carry

    lax.fori_loop(0, x_ref.shape[0] // LN_ROWS, rows, 0)


def _out_proj_kernel(a_ref, b_ref, w_ref, x_ref, g_ref, beta_ref, o_ref, ob_ref, *, nk, alpha):
    k = pl.program_id(1)
    half = nk // 2

    @pl.when(k == 0)
    def _():
        o_ref[...] = jnp.zeros_like(o_ref)

    def accumulate(lhs_ref):
        n = o_ref.shape[1]
        nb = min(n, 512)
        lhs = lhs_ref[...]
        for n0 in range(0, n, nb):
            o_ref[:, n0:n0 + nb] += jnp.dot(lhs, w_ref[:, n0:n0 + nb], preferred_element_type=F32)

    @pl.when(k < half)
    def _():
        accumulate(a_ref)

    @pl.when(k >= half)
    def _():
        accumulate(b_ref)

    @pl.when(k == nk - 1)
    def _():
        _deepnorm_epilogue(x_ref, o_ref, g_ref, beta_ref, alpha, (o_ref, ob_ref))


def _out_proj_ln(a, b, w, x, g, beta, *, bm, bk, alpha):
    m, ka = a.shape
    n = w.shape[1]
    half = ka // bk
    nk = 2 * half
    kern = functools.partial(_out_proj_kernel, nk=nk, alpha=alpha)
    return pl.pallas_call(
        kern,
        grid=(m // bm, nk),
        in_specs=[
            pl.BlockSpec((bm, bk), lambda i, k: (i, jnp.minimum(k, half - 1))),
            pl.BlockSpec((bm, bk), lambda i, k: (i, jnp.maximum(k - half, 0))),
            pl.BlockSpec((bk, n), lambda i, k: (k, 0)),
            pl.BlockSpec((bm, n), lambda i, k: (i, 0)),
            pl.BlockSpec((1, n), lambda i, k: (0, 0)),
            pl.BlockSpec((1, n), lambda i, k: (0, 0)),
        ],
        out_specs=[pl.BlockSpec((bm, n), lambda i, k: (i, 0)),
                   pl.BlockSpec((bm, n), lambda i, k: (i, 0))],
        out_shape=[jax.ShapeDtypeStruct((m, n), F32), jax.ShapeDtypeStruct((m, n), BF16)],
        compiler_params=pltpu.CompilerParams(
            dimension_semantics=("parallel", "arbitrary"), vmem_limit_bytes=VMEM_LIMIT),
        name="out_proj_ln",
    )(a, b, w, x, g.reshape(1, n), beta.reshape(1, n))


def _ffn_kernel(xb_ref, wg_ref, wu_ref, wd_ref, x_ref, g_ref, beta_ref, o_ref, *, nf, alpha):
    f = pl.program_id(1)

    @pl.when(f == 0)
    def _():
        o_ref[...] = jnp.zeros_like(o_ref)

    xb = xb_ref[...]
    gate = jnp.dot(xb, wg_ref[...], preferred_element_type=F32)
    up = jnp.dot(xb, wu_ref[...], preferred_element_type=F32)
    hid = (_silu(gate) * up).astype(BF16)
    dm = o_ref.shape[1]
    nb = min(dm, 512)
    for n0 in range(0, dm, nb):
        o_ref[:, n0:n0 + nb] += jnp.dot(hid, wd_ref[:, n0:n0 + nb], preferred_element_type=F32)

    @pl.when(f == nf - 1)
    def _():
        _deepnorm_epilogue(x_ref, o_ref, g_ref, beta_ref, alpha, (o_ref,))


def _ffn_ln(xb, x, wg, wu, wd, g, beta, *, bm, bf, alpha):
    m, dm = xb.shape
    ff = wg.shape[1]
    nf = ff // bf
    assert ff % bf == 0
    kern = functools.partial(_ffn_kernel, nf=nf, alpha=alpha)
    return pl.pallas_call(
        kern,
        grid=(m // bm, nf),
        in_specs=[
            pl.BlockSpec((bm, dm), lambda i, f: (i, 0)),
            pl.BlockSpec((dm, bf), lambda i, f: (0, f)),
            pl.BlockSpec((dm, bf), lambda i, f: (0, f)),
            pl.BlockSpec((bf, dm), lambda i, f: (f, 0)),
            pl.BlockSpec((bm, dm), lambda i, f: (i, 0)),
            pl.BlockSpec((1, dm), lambda i, f: (0, 0)),
            pl.BlockSpec((1, dm), lambda i, f: (0, 0)),
        ],
        out_specs=pl.BlockSpec((bm, dm), lambda i, f: (i, 0)),
        out_shape=jax.ShapeDtypeStruct((m, dm), F32),
        compiler_params=pltpu.CompilerParams(
            dimension_semantics=("parallel", "arbitrary"), vmem_limit_bytes=VMEM_LIMIT),
        name="ffn_ln",
    )(xb, wg, wu, wd, x, g.reshape(1, dm), beta.reshape(1, dm))


def _layer(x2, batch, seq, l, w_in, conv_w, a_log, dt_bias, gdn_norm_w, lq1, lk1, lq2, lk2,
           diff_norm_w, w_out, ln1_g, ln1_b, w_gate, w_up, w_down, ln2_g, ln2_b, alpha):
    m, dm = x2.shape
    n_gdn_heads = a_log.shape[0]
    gdn_w = n_gdn_heads * GDN_HEAD_DIM
    diff_w = w_out.shape[0] - gdn_w
    n_diff_heads = diff_w // DIFF_V_DIM
    lambda_init = 0.8 - 0.6 * math.exp(-0.3 * l)
    nc = seq // CHUNK

    bn = 768
    main = 4 * gdn_w
    ab = 2 * n_gdn_heads
    wide = -(-(main + ab) // bn) * bn
    w_t = w_in.T
    xb = x2.astype(BF16)

    proj_g = _matmul_wt32(xb, w_t, 0, wide, 1024, bn, F32, "proj_gdn")
    qk_w = 2 * n_diff_heads * 2 * DIFF_HEAD_DIM
    proj_qk, wd_b = _matmul_wt32(xb, w_t, main + ab, qk_w, 1024, 512, BF16, "proj_qk",
                                 side=(w_down, 256))
    proj_vt, wo_b = _matmul_wt32(xb, w_t, main + ab + qk_w, diff_w, 1024, 512, BF16, "proj_vt",
                                 transposed_out=True, side=(w_out, 128))

    gates = proj_g[:, main:main + ab].reshape(batch, seq, ab).transpose(0, 2, 1)
    gates = gates.reshape(batch, ab, nc, CHUNK)
    gates = jnp.pad(gates, ((0, 0), (0, 0), (0, 0), (0, LANES - CHUNK)))
    gdn_out = _gdn(proj_g, conv_w, gates[:, :n_gdn_heads], gates[:, n_gdn_heads:], a_log, dt_bias,
                   gdn_norm_w, batch=batch, seq=seq, n_heads=n_gdn_heads, heads=2)

    diff_out, wg_b, wu_b = _diff_attn(proj_qk, proj_vt, lq1, lk1, lq2, lk2, diff_norm_w, w_gate, w_up,
                                      batch=batch, seq=seq, n_heads=n_diff_heads, tq=512, tk=512,
                                      lambda_init=lambda_init)

    x1, x1b = _out_proj_ln(gdn_out, diff_out, wo_b, x2, ln1_g, ln1_b, bm=512, bk=512, alpha=alpha)
    return _ffn_ln(x1b, x1, wg_b, wu_b, wd_b, ln2_g, ln2_b, bm=512, bf=256, alpha=alpha)


def kernel(x, w_in, conv_w, a_log, dt_bias, gdn_norm_w, lambda_q1, lambda_k1, lambda_q2, lambda_k2,
           diff_norm_w, w_out, ln1_g, ln1_b, w_gate, w_up, w_down, ln2_g, ln2_b):
    batch, seq, dm = x.shape
    depth = w_in.shape[0]
    alpha = (2 * depth) ** 0.25
    x2 = x.reshape(batch * seq, dm)
    for l in range(depth):
        x2 = _layer(x2, batch, seq, l, w_in[l], conv_w[l], a_log[l], dt_bias[l], gdn_norm_w[l],
                    lambda_q1[l], lambda_k1[l], lambda_q2[l], lambda_k2[l], diff_norm_w[l], w_out[l],
                    ln1_g[l], ln1_b[l], w_gate[l], w_up[l], w_down[l], ln2_g[l], ln2_b[l], alpha)
    return x2.reshape(batch, seq, dm)
```

```python
import functools
import math

import jax
import jax.numpy as jnp
from jax import lax
from jax.experimental import pallas as pl
from jax.experimental.pallas import tpu as pltpu

F32 = jnp.float32
BF16 = jnp.bfloat16

GDN_HEAD_DIM = 128
DIFF_HEAD_DIM = 128
DIFF_V_DIM = 256
CONV_K = 4
CHUNK = 64
PROLOGUE_ROWS = 128
LANES = 128
VMEM_LIMIT = 60 * 1024 * 1024


def _dot(a, b):
    return jnp.dot(a.astype(BF16), b.astype(BF16), preferred_element_type=F32)


def _dot_nt(a, b):
    return lax.dot_general(a.astype(BF16), b.astype(BF16), (((1,), (1,)), ((), ())),
                           preferred_element_type=F32)


def _dot_tn(a, b):
    return lax.dot_general(a.astype(BF16), b.astype(BF16), (((0,), (0,)), ((), ())),
                           preferred_element_type=F32)


def _silu(x):
    return x * jax.nn.sigmoid(x)


CAST_ROWS = 128


SIDE_COLS = 1024


def _cast_block(src_ref, dst_ref, col0=0, col1=None):
    col1 = src_ref.shape[1] if col1 is None else col1
    for c0 in range(col0, col1, SIDE_COLS):
        c1 = min(c0 + SIDE_COLS, col1)
        dst_ref[:, c0:c1] = src_ref[:, c0:c1].astype(BF16)


def _mm_wt32_kernel(x_ref, w_ref, *rest, transposed_out, has_side):
    if has_side:
        side_ref, o_ref, side_out_ref, wb_ref = rest
        _cast_block(side_ref, side_out_ref)
    else:
        o_ref, wb_ref = rest

    @pl.when(pl.program_id(1) == 0)
    def _():
        for r in range(0, w_ref.shape[0], CAST_ROWS):
            wb_ref[r:r + CAST_ROWS, :] = w_ref[r:r + CAST_ROWS, :].astype(BF16)

    if transposed_out:
        o_ref[...] = _dot_nt(wb_ref[...], x_ref[...]).astype(o_ref.dtype)
    else:
        o_ref[...] = _dot_nt(x_ref[...], wb_ref[...]).astype(o_ref.dtype)


def _matmul_wt32(x, w_t, row0, n_out, bm, bn, out_dtype, name, transposed_out=False, side=None):
    m, k = x.shape
    assert m % bm == 0 and n_out % bn == 0 and bn % CAST_ROWS == 0 and row0 % 8 == 0
    assert row0 + n_out <= w_t.shape[0] and k == w_t.shape[1]
    nj, ni = n_out // bn, m // bm
    if transposed_out:
        out_specs = [pl.BlockSpec((bn, bm), lambda j, i: (j, i))]
        out_shape = [jax.ShapeDtypeStruct((n_out, m), out_dtype)]
    else:
        out_specs = [pl.BlockSpec((bm, bn), lambda j, i: (i, j))]
        out_shape = [jax.ShapeDtypeStruct((m, n_out), out_dtype)]
    in_specs = [pl.BlockSpec((bm, k), lambda j, i: (i, 0)),
                pl.BlockSpec((pl.Element(bn), pl.Element(k)), lambda j, i: ((row0 // 8 + j * (bn // 8)) * 8, 0))]
    args = [x, w_t]
    if side is not None:
        a, rows = side
        n_slabs = a.shape[0] // rows
        assert a.shape[0] % rows == 0 and rows % 16 == 0 and n_slabs <= nj * ni
        slab = lambda j, i: (jnp.minimum(j * ni + i, n_slabs - 1), 0)
        in_specs.append(pl.BlockSpec((rows, a.shape[1]), slab))
        out_specs.append(pl.BlockSpec((rows, a.shape[1]), slab))
        out_shape.append(jax.ShapeDtypeStruct(a.shape, BF16))
        args.append(a)
    outs = pl.pallas_call(
        functools.partial(_mm_wt32_kernel, transposed_out=transposed_out, has_side=side is not None),
        grid=(nj, ni),
        in_specs=in_specs,
        out_specs=out_specs,
        out_shape=out_shape,
        scratch_shapes=[pltpu.VMEM((bn, k), BF16)],
        compiler_params=pltpu.CompilerParams(
            dimension_semantics=("arbitrary", "arbitrary"), vmem_limit_bytes=VMEM_LIMIT),
        name=name,
    )(*args)
    return outs if side is not None else outs[0]


def _gdn_kernel(q_ref, k_ref, v_ref, z_ref, cwq_ref, cwk_ref, cwv_ref, ga_ref, gb_ref,
                alog_ref, dtb_ref, nw_ref, o_ref,
                xpad, qs, ks, vs, q2s, k2s, o0s, rs, gcs, bts, egs, *, heads, seq, unroll1, unroll2):
    d = GDN_HEAD_DIM
    c_ = CHUNK
    nc = seq // c_
    row = lax.broadcasted_iota(jnp.int32, (c_, c_), 0)
    col = lax.broadcasted_iota(jnp.int32, (c_, c_), 1)
    tril = col <= row
    strict = col < row
    eye = col == row
    col2 = lax.broadcasted_iota(jnp.int32, (c_, 2 * c_), 1)
    right = col2 >= c_
    eye_r = jnp.where(col2 == lax.broadcasted_iota(jnp.int32, (c_, 2 * c_), 0) + c_, 1.0, 0.0)
    lane = lax.broadcasted_iota(jnp.int32, (nc, LANES), 1)

    rb = min(seq, PROLOGUE_ROWS)
    halo = CONV_K - 1
    n_blocks = nc // unroll1
    block_rows = unroll1 * c_

    def aligned(x, m):
        return x if isinstance(x, int) else pl.multiple_of(x, m)

    def drip(tasks, n_calls):
        it = iter(tasks)
        per = -(-len(tasks) // max(n_calls, 1))

        def hook():
            for _ in range(per):
                t = next(it, None)
                if t is not None:
                    t()

        def flush():
            for t in it:
                t()

        return hook, flush

    def conv_rows(x_ref, cw_ref, dst, post, g, r0):
        cols = slice(g * d, (g + 1) * d)
        w = cw_ref[:, cols]
        if r0 == 0:
            xpad[0:8, :] = jnp.zeros((8, d), F32)
            xpad[8:8 + rb, :] = x_ref[0:rb, cols]
        acc = None
        for j in range(CONV_K):
            if r0 == 0:
                xs = xpad[8 - halo + j:8 - halo + j + rb, :]
            else:
                xs = x_ref[r0 - halo + j:r0 - halo + j + rb, cols]
            t = xs * w[j:j + 1, :]
            acc = t if acc is None else acc + t
        dst[g, r0:r0 + rb, :] = post(_silu(acc))

    l2n = lambda y: y * lax.rsqrt(jnp.sum(y * y, -1, keepdims=True) + 1e-6)
    streams = ((q_ref, cwq_ref, qs, lambda y: l2n(y) * (d ** -0.5)),
               (k_ref, cwk_ref, ks, l2n),
               (v_ref, cwv_ref, vs, lambda y: y))
    prologue = [(r0, functools.partial(conv_rows, *strm, g, r0))
                for r0 in range(0, seq, rb) for g in range(heads) for strm in streams]

    for g in range(heads):
        a = ga_ref[0, g] + dtb_ref[g]
        sp = jnp.maximum(a, 0.0) + jnp.log1p(jnp.exp(-jnp.abs(a)))
        gl = jnp.where(lane < c_, -jnp.exp(alog_ref[g]) * sp, 0.0)
        sh = 1
        while sh < c_:
            gl = gl + jnp.where(lane >= sh, pltpu.roll(gl, sh, 1), 0.0)
            sh *= 2
        gcs[g] = gl
        bts[g] = jax.nn.sigmoid(gb_ref[0, g])

    for r0, task in prologue:
        if r0 < block_rows:
            task()

    def phase1(cg, between_levels):
        chains = [(cg * unroll1 + j, g) for j in range(unroll1) for g in range(heads)]
        st = []
        for c, g in chains:
            r0 = aligned(c * c_, c_)
            qc = qs[g, pl.ds(r0, c_), :]
            kc = ks[g, pl.ds(r0, c_), :]
            vc = vs[g, pl.ds(r0, c_), :]
            gcr = gcs[g, pl.ds(c, 1), :][:, :c_]
            btr = bts[g, pl.ds(c, 1), :][:, :c_]
            gcol = jnp.sum(jnp.where(eye, jnp.broadcast_to(gcr, (c_, c_)), 0.0), 1, keepdims=True)
            bcol = jnp.sum(jnp.where(eye, jnp.broadcast_to(btr, (c_, c_)), 0.0), 1, keepdims=True)
            glast = gcr[:, c_ - 1:c_]
            decay = jnp.where(tril, jnp.exp(jnp.where(tril, gcol - gcr, 0.0)), 0.0)
            kb = kc * bcol
            r = jnp.concatenate([vc * bcol, kb * jnp.exp(gcol)], axis=1)
            st.append(dict(c=c, g=g, r0=r0, qc=qc, kc=kc, kb=kb, gcol=gcol, glast=glast, decay=decay, r=r))
        for t in st:
            t["aq"] = _dot_nt(jnp.concatenate([t["kb"], t["qc"]], axis=0), t["kc"])
        between_levels()
        for t in st:
            t["nl"] = jnp.where(strict, -(t["aq"][:c_] * t["decay"]), 0.0)
            t["intra"] = jnp.where(tril, t["aq"][c_:] * t["decay"], 0.0)
        n_levels = c_.bit_length() - 1
        for t in st:
            nt0 = jnp.concatenate([t["nl"], jnp.zeros((c_, c_), F32)], axis=1) + eye_r
            t["nt"] = _dot(t["nl"], nt0) + eye_r
        between_levels()
        for lvl in range(1, n_levels):
            for t in st:
                nt = t["nt"]
                t["nt"] = _dot(nt[:, :c_], nt) + jnp.where(right, nt, 0.0)
            between_levels()
        for t in st:
            t["r"] = _dot(t["nt"], jnp.concatenate([jnp.zeros_like(t["r"]), t["r"]], axis=0))
        between_levels()
        for t in st:
            kd = t["kc"] * jnp.exp(t["glast"] - t["gcol"])
            t["kq"] = _dot_tn(kd, t["r"])
            t["io"] = _dot(t["intra"], t["r"])
        between_levels()
        for t in st:
            c, g, r0 = t["c"], t["g"], t["r0"]
            rd = aligned(c * d, d)
            q2s[g, pl.ds(rd, d), :] = t["kq"][:, :d]
            k2s[g, pl.ds(rd, d), :] = t["kq"][:, d:].astype(BF16)
            o0s[g, pl.ds(r0, c_), :] = t["io"][:, :d]
            rs[g, pl.ds(r0, c_), :] = (t["qc"] * jnp.exp(t["gcol"]) - t["io"][:, d:]).astype(BF16)
            egs[g, pl.ds(c, 1), :] = jnp.broadcast_to(jnp.exp(t["glast"]), (1, LANES))

    def phase2(c, states):
        r0 = aligned(c * c_, c_)
        rd = aligned(c * d, d)
        sbs = [states[g].astype(BF16) for g in range(heads)]
        ks2 = [_dot(k2s[g, pl.ds(rd, d), :], sbs[g]) for g in range(heads)]
        ros = [_dot(rs[g, pl.ds(r0, c_), :], sbs[g]) for g in range(heads)]
        new_states = [states[g] * egs[g, pl.ds(c, 1), :] + q2s[g, pl.ds(rd, d), :] - ks2[g]
                      for g in range(heads)]
        for g in range(heads):
            o0s[g, pl.ds(r0, c_), :] += ros[g]
        return tuple(new_states)

    def norm_rows(g, r0):
        cols = slice(g * d, (g + 1) * d)
        o = o0s[g, r0:r0 + rb, :]
        on = o * lax.rsqrt(jnp.mean(o * o, -1, keepdims=True) + 1e-6) * nw_ref[...]
        o_ref[r0:r0 + rb, cols] = (on * _silu(z_ref[r0:r0 + rb, cols])).astype(o_ref.dtype)

    epilogue = [(r0, functools.partial(norm_rows, g, r0)) for r0 in range(0, seq, rb) for g in range(heads)]

    n_hooks = c_.bit_length() + 3
    hook, flush = drip([t for r0, t in prologue if r0 >= block_rows], n_hooks)
    phase1(0, hook)
    flush()

    per_hook = -(-unroll1 // n_hooks)

    def pipelined(blk, states):
        box, done = [states], [0]

        def recurrence_steps():
            for _ in range(per_hook):
                if done[0] < unroll1:
                    box[0] = phase2((blk - 1) * unroll1 + done[0], box[0])
                    done[0] += 1

        phase1(blk, recurrence_steps)
        while done[0] < unroll1:
            recurrence_steps()
        return box[0]

    states = lax.fori_loop(1, n_blocks, pipelined, tuple(jnp.zeros((d, d), F32) for _ in range(heads)))

    last0 = (n_blocks - 1) * unroll1
    hook, flush = drip([t for r0, t in epilogue if r0 < last0 * c_], nc - last0)
    for c in range(last0, nc):
        states = phase2(c, states)
        hook()
    flush()
    for r0, task in epilogue:
        if r0 >= last0 * c_:
            task()


def _gdn(proj, conv_w, ga_rows, gb_rows, a_log, dt_bias, norm_w, *, batch, seq, n_heads, heads):
    d = GDN_HEAD_DIM
    gw = heads * d
    hb = n_heads // heads
    nc = seq // CHUNK
    m = batch * seq
    kern = functools.partial(_gdn_kernel, heads=heads, seq=seq, unroll1=math.gcd(nc, 16), unroll2=2)
    col = lambda off: (lambda b, h: (b, off * hb + h))
    cw = lambda off: (lambda b, h: (0, off * hb + h))
    return pl.pallas_call(
        kern,
        grid=(batch, hb),
        in_specs=[
            pl.BlockSpec((seq, gw), col(0)), pl.BlockSpec((seq, gw), col(1)),
            pl.BlockSpec((seq, gw), col(2)), pl.BlockSpec((seq, gw), col(3)),
            pl.BlockSpec((CONV_K, gw), cw(0)), pl.BlockSpec((CONV_K, gw), cw(1)),
            pl.BlockSpec((CONV_K, gw), cw(2)),
            pl.BlockSpec((1, heads, nc, LANES), lambda b, h: (b, h, 0, 0)),
            pl.BlockSpec((1, heads, nc, LANES), lambda b, h: (b, h, 0, 0)),
            pl.BlockSpec((heads, 1, 1), lambda b, h: (h, 0, 0)),
            pl.BlockSpec((heads, 1, 1), lambda b, h: (h, 0, 0)),
            pl.BlockSpec((1, d), lambda b, h: (0, 0)),
        ],
        out_specs=pl.BlockSpec((seq, gw), lambda b, h: (b, h)),
        out_shape=jax.ShapeDtypeStruct((m, n_heads * d), BF16),
        scratch_shapes=[
            pltpu.VMEM((PROLOGUE_ROWS + 8, d), F32),
            pltpu.VMEM((heads, seq, d), F32),
            pltpu.VMEM((heads, seq, d), F32),
            pltpu.VMEM((heads, seq, d), F32),
            pltpu.VMEM((heads, nc * d, d), F32),
            pltpu.VMEM((heads, nc * d, d), BF16),
            pltpu.VMEM((heads, seq, d), F32),
            pltpu.VMEM((heads, seq, d), BF16),
            pltpu.VMEM((heads, nc, LANES), F32),
            pltpu.VMEM((heads, nc, LANES), F32),
            pltpu.VMEM((heads, nc, LANES), F32),
        ],
        compiler_params=pltpu.CompilerParams(
            dimension_semantics=("parallel", "parallel"), vmem_limit_bytes=VMEM_LIMIT),
        name="gdn",
    )(proj, proj, proj, proj, conv_w, conv_w, conv_w, ga_rows, gb_rows,
      a_log.reshape(n_heads, 1, 1), dt_bias.reshape(n_heads, 1, 1), norm_w.reshape(1, d))


def _diff_attn_kernel(q_ref, k_ref, vt_ref, lq1_ref, lk1_ref, lq2_ref, lk2_ref, nw_ref, sa_ref, sb_ref,
                      o_ref, sa_out_ref, sb_out_ref, acc1, acc2, *, tq, tk, seq, lambda_init):
    dh = DIFF_HEAD_DIM
    c2 = (dh ** -0.5) * math.log2(math.e)
    neg = float(jnp.finfo(jnp.float32).min)
    krow = lax.broadcasted_iota(jnp.int32, (tk, tq), 0)
    qcol = lax.broadcasted_iota(jnp.int32, (tk, tq), 1)
    lam = (jnp.exp(jnp.sum(lq1_ref[...] * lk1_ref[...], -1, keepdims=True))
           - jnp.exp(jnp.sum(lq2_ref[...] * lk2_ref[...], -1, keepdims=True)) + lambda_init)
    nw = nw_ref[...] * (1.0 - lambda_init)

    nq = seq // tq
    side_cols = sa_ref.shape[1]
    piece = -(-side_cols // (nq * LANES)) * LANES
    for i in range(nq):
        for src, dst in ((sa_ref, sa_out_ref), (sb_ref, sb_out_ref)):
            if i * piece < side_cols:
                _cast_block(src, dst, i * piece, min((i + 1) * piece, side_cols))
        q1 = q_ref[i * tq:(i + 1) * tq, :dh]
        q2 = q_ref[i * tq:(i + 1) * tq, dh:]
        m1 = m2 = jnp.full((1, tq), neg, F32)
        l1 = l2 = jnp.zeros((1, tq), F32)
        for j in range((i + 1) * tq // tk):
            kj = k_ref[j * tk:(j + 1) * tk, :]
            vt = vt_ref[:, j * tk:(j + 1) * tk]
            s1 = _dot_nt(kj[:, :dh], q1)
            s2 = _dot_nt(kj[:, dh:], q2)
            if (j + 1) * tk - 1 > i * tq:
                keep = krow + j * tk <= qcol + i * tq
                s1 = jnp.where(keep, s1, neg)
                s2 = jnp.where(keep, s2, neg)
            m1n = jnp.maximum(m1, jnp.max(s1, 0, keepdims=True))
            a1 = jnp.exp2((m1 - m1n) * c2)
            p1 = jnp.exp2((s1 - m1n) * c2)
            l1 = a1 * l1 + jnp.sum(p1, 0, keepdims=True)
            pv1 = _dot(vt, p1)
            m2n = jnp.maximum(m2, jnp.max(s2, 0, keepdims=True))
            a2 = jnp.exp2((m2 - m2n) * c2)
            p2 = jnp.exp2((s2 - m2n) * c2)
            l2 = a2 * l2 + jnp.sum(p2, 0, keepdims=True)
            pv2 = _dot(vt, p2)
            if j == 0:
                acc1[...] = pv1
                acc2[...] = pv2
            else:
                acc1[...] = acc1[...] * a1 + pv1
                acc2[...] = acc2[...] * a2 + pv2
            m1, m2 = m1n, m2n
        o = acc1[...] / l1 - lam * (acc2[...] / l2)
        on = o * lax.rsqrt(jnp.mean(o * o, 0, keepdims=True) + 1e-6) * nw
        o_ref[i * tq:(i + 1) * tq, :] = on.T.astype(o_ref.dtype)


def _diff_attn(qk, vt, lq1, lk1, lq2, lk2, norm_w, side_a, side_b, *, batch, seq, n_heads, tq, tk, lambda_init):
    dv = DIFF_V_DIM
    m = batch * seq
    steps = batch * n_heads
    assert side_a.shape == side_b.shape and side_a.shape[0] % (16 * steps) == 0
    srows, scols = side_a.shape[0] // steps, side_a.shape[1]
    slab = pl.BlockSpec((srows, scols), lambda b, h: (b * n_heads + h, 0))
    assert seq % tq == 0 and tq % tk == 0
    kern = functools.partial(_diff_attn_kernel, tq=tq, tk=tk, seq=seq, lambda_init=lambda_init)
    vec = lambda n: pl.BlockSpec((1, n), lambda b, h: (0, 0))
    return pl.pallas_call(
        kern,
        grid=(batch, n_heads),
        in_specs=[
            pl.BlockSpec((seq, dv), lambda b, h: (b, h)),
            pl.BlockSpec((seq, dv), lambda b, h: (b, n_heads + h)),
            pl.BlockSpec((dv, seq), lambda b, h: (h, b)),
            vec(DIFF_HEAD_DIM), vec(DIFF_HEAD_DIM), vec(DIFF_HEAD_DIM), vec(DIFF_HEAD_DIM),
            pl.BlockSpec((dv, 1), lambda b, h: (0, 0)),
            slab, slab,
        ],
        out_specs=[pl.BlockSpec((seq, dv), lambda b, h: (b, h)), slab, slab],
        out_shape=[jax.ShapeDtypeStruct((m, n_heads * dv), BF16),
                   jax.ShapeDtypeStruct(side_a.shape, BF16), jax.ShapeDtypeStruct(side_b.shape, BF16)],
        scratch_shapes=[pltpu.VMEM((dv, tq), F32), pltpu.VMEM((dv, tq), F32)],
        compiler_params=pltpu.CompilerParams(
            dimension_semantics=("parallel", "parallel"), vmem_limit_bytes=VMEM_LIMIT),
        name="diff_attn",
    )(qk, qk, vt, lq1.reshape(1, -1), lk1.reshape(1, -1), lq2.reshape(1, -1),
      lk2.reshape(1, -1), norm_w.reshape(-1, 1), side_a, side_b)


def _layer_norm(y, g, b):
    mu = jnp.mean(y, -1, keepdims=True)
    yc = y - mu
    var = jnp.mean(yc * yc, -1, keepdims=True)
    return yc * lax.rsqrt(var + 1e-5) * g + b


LN_ROWS = 64


def _deepnorm_epilogue(x_ref, acc_ref, g_ref, beta_ref, alpha, out_refs):
    g = g_ref[...]
    beta = beta_ref[...]

    def rows(i, carry):
        r0 = pl.multiple_of(i * LN_ROWS, LN_ROWS)
        sl = pl.ds(r0, LN_ROWS)
        y = _layer_norm(alpha * x_ref[sl, :] + acc_ref[sl, :], g, beta)
        for o in out_refs:
            o[sl, :] = y.astype(o.dtype)
        return carry

    lax.fori_loop(0, x_ref.shape[0] // LN_ROWS, rows, 0)


def _out_proj_kernel(a_ref, b_ref, w_ref, x_ref, g_ref, beta_ref, o_ref, ob_ref, *, nk, alpha):
    k = pl.program_id(1)
    half = nk // 2

    @pl.when(k == 0)
    def _():
        o_ref[...] = jnp.zeros_like(o_ref)

    def accumulate(lhs_ref):
        n = o_ref.shape[1]
        nb = min(n, 512)
        lhs = lhs_ref[...]
        for n0 in range(0, n, nb):
            o_ref[:, n0:n0 + nb] += jnp.dot(lhs, w_ref[:, n0:n0 + nb], preferred_element_type=F32)

    @pl.when(k < half)
    def _():
        accumulate(a_ref)

    @pl.when(k >= half)
    def _():
        accumulate(b_ref)

    @pl.when(k == nk - 1)
    def _():
        _deepnorm_epilogue(x_ref, o_ref, g_ref, beta_ref, alpha, (o_ref, ob_ref))


def _out_proj_ln(a, b, w, x, g, beta, *, bm, bk, alpha):
    m, ka = a.shape
    n = w.shape[1]
    half = ka // bk
    nk = 2 * half
    kern = functools.partial(_out_proj_kernel, nk=nk, alpha=alpha)
    return pl.pallas_call(
        kern,
        grid=(m // bm, nk),
        in_specs=[
            pl.BlockSpec((bm, bk), lambda i, k: (i, jnp.minimum(k, half - 1))),
            pl.BlockSpec((bm, bk), lambda i, k: (i, jnp.maximum(k - half, 0))),
            pl.BlockSpec((bk, n), lambda i, k: (k, 0)),
            pl.BlockSpec((bm, n), lambda i, k: (i, 0)),
            pl.BlockSpec((1, n), lambda i, k: (0, 0)),
            pl.BlockSpec((1, n), lambda i, k: (0, 0)),
        ],
        out_specs=[pl.BlockSpec((bm, n), lambda i, k: (i, 0)),
                   pl.BlockSpec((bm, n), lambda i, k: (i, 0))],
        out_shape=[jax.ShapeDtypeStruct((m, n), F32), jax.ShapeDtypeStruct((m, n), BF16)],
        compiler_params=pltpu.CompilerParams(
            dimension_semantics=("parallel", "arbitrary"), vmem_limit_bytes=VMEM_LIMIT),
        name="out_proj_ln",
    )(a, b, w, x, g.reshape(1, n), beta.reshape(1, n))


def _ffn_kernel(xb_ref, wg_ref, wu_ref, wd_ref, x_ref, g_ref, beta_ref, o_ref, *, nf, alpha):
    f = pl.program_id(1)

    @pl.when(f == 0)
    def _():
        o_ref[...] = jnp.zeros_like(o_ref)

    xb = xb_ref[...]
    gate = jnp.dot(xb, wg_ref[...], preferred_element_type=F32)
    up = jnp.dot(xb, wu_ref[...], preferred_element_type=F32)
    hid = (_silu(gate) * up).astype(BF16)
    dm = o_ref.shape[1]
    nb = min(dm, 512)
    for n0 in range(0, dm, nb):
        o_ref[:, n0:n0 + nb] += jnp.dot(hid, wd_ref[:, n0:n0 + nb], preferred_element_type=F32)

    @pl.when(f == nf - 1)
    def _():
        _deepnorm_epilogue(x_ref, o_ref, g_ref, beta_ref, alpha, (o_ref,))


def _ffn_ln(xb, x, wg, wu, wd, g, beta, *, bm, bf, alpha):
    m, dm = xb.shape
    ff = wg.shape[1]
    nf = ff // bf
    assert ff % bf == 0
    kern = functools.partial(_ffn_kernel, nf=nf, alpha=alpha)
    return pl.pallas_call(
        kern,
        grid=(m // bm, nf),
        in_specs=[
            pl.BlockSpec((bm, dm), lambda i, f: (i, 0)),
            pl.BlockSpec((dm, bf), lambda i, f: (0, f)),
            pl.BlockSpec((dm, bf), lambda i, f: (0, f)),
            pl.BlockSpec((bf, dm), lambda i, f: (f, 0)),
            pl.BlockSpec((bm, dm), lambda i, f: (i, 0)),
            pl.BlockSpec((1, dm), lambda i, f: (0, 0)),
            pl.BlockSpec((1, dm), lambda i, f: (0, 0)),
        ],
        out_specs=pl.BlockSpec((bm, dm), lambda i, f: (i, 0)),
        out_shape=jax.ShapeDtypeStruct((m, dm), F32),
        compiler_params=pltpu.CompilerParams(
            dimension_semantics=("parallel", "arbitrary"), vmem_limit_bytes=VMEM_LIMIT),
        name="ffn_ln",
    )(xb, wg, wu, wd, x, g.reshape(1, dm), beta.reshape(1, dm))


def _layer(x2, batch, seq, l, w_in, conv_w, a_log, dt_bias, gdn_norm_w, lq1, lk1, lq2, lk2,
           diff_norm_w, w_out, ln1_g, ln1_b, w_gate, w_up, w_down, ln2_g, ln2_b, alpha):
    m, dm = x2.shape
    n_gdn_heads = a_log.shape[0]
    gdn_w = n_gdn_heads * GDN_HEAD_DIM
    diff_w = w_out.shape[0] - gdn_w
    n_diff_heads = diff_w // DIFF_V_DIM
    lambda_init = 0.8 - 0.6 * math.exp(-0.3 * l)
    nc = seq // CHUNK

    bn = 768
    main = 4 * gdn_w
    ab = 2 * n_gdn_heads
    wide = -(-(main + ab) // bn) * bn
    w_t = w_in.T
    xb = x2.astype(BF16)

    proj_g = _matmul_wt32(xb, w_t, 0, wide, 1024, bn, F32, "proj_gdn")
    qk_w = 2 * n_diff_heads * 2 * DIFF_HEAD_DIM
    proj_qk, wd_b = _matmul_wt32(xb, w_t, main + ab, qk_w, 1024, 512, BF16, "proj_qk",
                                 side=(w_down, 256))
    proj_vt, wo_b = _matmul_wt32(xb, w_t, main + ab + qk_w, diff_w, 1024, 512, BF16, "proj_vt",
                                 transposed_out=True, side=(w_out, 128))

    gates = proj_g[:, main:main + ab].reshape(batch, seq, ab).transpose(0, 2, 1)
    gates = gates.reshape(batch, ab, nc, CHUNK)
    gates = jnp.pad(gates, ((0, 0), (0, 0), (0, 0), (0, LANES - CHUNK)))
    gdn_out = _gdn(proj_g, conv_w, gates[:, :n_gdn_heads], gates[:, n_gdn_heads:], a_log, dt_bias,
                   gdn_norm_w, batch=batch, seq=seq, n_heads=n_gdn_heads, heads=2)

    diff_out, wg_b, wu_b = _diff_attn(proj_qk, proj_vt, lq1, lk1, lq2, lk2, diff_norm_w, w_gate, w_up,
                                      batch=batch, seq=seq, n_heads=n_diff_heads, tq=512, tk=512,
                                      lambda_init=lambda_init)

    x1, x1b = _out_proj_ln(gdn_out, diff_out, wo_b, x2, ln1_g, ln1_b, bm=512, bk=512, alpha=alpha)
    return _ffn_ln(x1b, x1, wg_b, wu_b, wd_b, ln2_g, ln2_b, bm=512, bf=256, alpha=alpha)


def kernel(x, w_in, conv_w, a_log, dt_bias, gdn_norm_w, lambda_q1, lambda_k1, lambda_q2, lambda_k2,
           diff_norm_w, w_out, ln1_g, ln1_b, w_gate, w_up, w_down, ln2_g, ln2_b):
    batch, seq, dm = x.shape
    depth = w_in.shape[0]
    alpha = (2 * depth) ** 0.25
    x2 = x.reshape(batch * seq, dm)
    for l in range(depth):
        x2 = _layer(x2, batch, seq, l, w_in[l], conv_w[l], a_log[l], dt_bias[l], gdn_norm_w[l],
                    lambda_q1[l], lambda_k1[l], lambda_q2[l], lambda_k2[l], diff_norm_w[l], w_out[l],
                    ln1_g[l], ln1_b[l], w_gate[l], w_up[l], w_down[l], ln2_g[l], ln2_b[l], alpha)
    return x2.reshape(batch, seq, dm)
```

```python
import functools
import math

import jax
import jax.numpy as jnp
from jax import lax
from jax.experimental import pallas as pl
from jax.experimental.pallas import tpu as pltpu

F32 = jnp.float32
BF16 = jnp.bfloat16

GDN_HEAD_DIM = 128
DIFF_HEAD_DIM = 128
DIFF_V_DIM = 256
CONV_K = 4
CHUNK = 64
PROLOGUE_ROWS = 128
LANES = 128
VMEM_LIMIT = 60 * 1024 * 1024


def _dot(a, b):
    return jnp.dot(a.astype(BF16), b.astype(BF16), preferred_element_type=F32)


def _dot_nt(a, b):
    return lax.dot_general(a.astype(BF16), b.astype(BF16), (((1,), (1,)), ((), ())),
                           preferred_element_type=F32)


def _dot_tn(a, b):
    return lax.dot_general(a.astype(BF16), b.astype(BF16), (((0,), (0,)), ((), ())),
                           preferred_element_type=F32)


def _silu(x):
    return x * jax.nn.sigmoid(x)


CAST_ROWS = 128


SIDE_COLS = 1024


def _cast_block(src_ref, dst_ref, col0=0, col1=None):
    col1 = src_ref.shape[1] if col1 is None else col1
    for c0 in range(col0, col1, SIDE_COLS):
        c1 = min(c0 + SIDE_COLS, col1)
        dst_ref[:, c0:c1] = src_ref[:, c0:c1].astype(BF16)


def _mm_wt32_kernel(x_ref, w_ref, *rest, transposed_out, has_side):
    if has_side:
        side_ref, o_ref, side_out_ref, wb_ref = rest
        _cast_block(side_ref, side_out_ref)
    else:
        o_ref, wb_ref = rest

    @pl.when(pl.program_id(1) == 0)
    def _():
        for r in range(0, w_ref.shape[0], CAST_ROWS):
            wb_ref[r:r + CAST_ROWS, :] = w_ref[r:r + CAST_ROWS, :].astype(BF16)

    if transposed_out:
        o_ref[...] = _dot_nt(wb_ref[...], x_ref[...]).astype(o_ref.dtype)
    else:
        o_ref[...] = _dot_nt(x_ref[...], wb_ref[...]).astype(o_ref.dtype)


def _matmul_wt32(x, w_t, row0, n_out, bm, bn, out_dtype, name, transposed_out=False, side=None):
    m, k = x.shape
    assert m % bm == 0 and n_out % bn == 0 and bn % CAST_ROWS == 0 and row0 % 8 == 0
    assert row0 + n_out <= w_t.shape[0] and k == w_t.shape[1]
    nj, ni = n_out // bn, m // bm
    if transposed_out:
        out_specs = [pl.BlockSpec((bn, bm), lambda j, i: (j, i))]
        out_shape = [jax.ShapeDtypeStruct((n_out, m), out_dtype)]
    else:
        out_specs = [pl.BlockSpec((bm, bn), lambda j, i: (i, j))]
        out_shape = [jax.ShapeDtypeStruct((m, n_out), out_dtype)]
    in_specs = [pl.BlockSpec((bm, k), lambda j, i: (i, 0)),
                pl.BlockSpec((pl.Element(bn), pl.Element(k)), lambda j, i: ((row0 // 8 + j * (bn // 8)) * 8, 0))]
    args = [x, w_t]
    if side is not None:
        a, rows = side
        n_slabs = a.shape[0] // rows
        assert a.shape[0] % rows == 0 and rows % 16 == 0 and n_slabs <= nj * ni
        slab = lambda j, i: (jnp.minimum(j * ni + i, n_slabs - 1), 0)
        in_specs.append(pl.BlockSpec((rows, a.shape[1]), slab))
        out_specs.append(pl.BlockSpec((rows, a.shape[1]), slab))
        out_shape.append(jax.ShapeDtypeStruct(a.shape, BF16))
        args.append(a)
    outs = pl.pallas_call(
        functools.partial(_mm_wt32_kernel, transposed_out=transposed_out, has_side=side is not None),
        grid=(nj, ni),
        in_specs=in_specs,
        out_specs=out_specs,
        out_shape=out_shape,
        scratch_shapes=[pltpu.VMEM((bn, k), BF16)],
        compiler_params=pltpu.CompilerParams(
            dimension_semantics=("arbitrary", "arbitrary"), vmem_limit_bytes=VMEM_LIMIT),
        name=name,
    )(*args)
    return outs if side is not None else outs[0]


def _gdn_kernel(q_ref, k_ref, v_ref, z_ref, cwq_ref, cwk_ref, cwv_ref, ga_ref, gb_ref,
                alog_ref, dtb_ref, nw_ref, side_ref, o_ref, side_out_ref,
                xpad, qs, ks, vs, q2s, k2s, o0s, rs, gcs, bts, egs, *, heads, seq, unroll1, unroll2):
    d = GDN_HEAD_DIM
    c_ = CHUNK
    nc = seq // c_
    row = lax.broadcasted_iota(jnp.int32, (c_, c_), 0)
    col = lax.broadcasted_iota(jnp.int32, (c_, c_), 1)
    tril = col <= row
    strict = col < row
    eye = col == row
    col2 = lax.broadcasted_iota(jnp.int32, (c_, 2 * c_), 1)
    right = col2 >= c_
    eye_r = jnp.where(col2 == lax.broadcasted_iota(jnp.int32, (c_, 2 * c_), 0) + c_, 1.0, 0.0)
    lane = lax.broadcasted_iota(jnp.int32, (nc, LANES), 1)

    rb = min(seq, PROLOGUE_ROWS)
    halo = CONV_K - 1
    n_blocks = nc // unroll1
    block_rows = unroll1 * c_

    def aligned(x, m):
        return x if isinstance(x, int) else pl.multiple_of(x, m)

    def drip(tasks, n_calls):
        it = iter(tasks)
        per = -(-len(tasks) // max(n_calls, 1))

        def hook():
            for _ in range(per):
                t = next(it, None)
                if t is not None:
                    t()

        def flush():
            for t in it:
                t()

        return hook, flush

    def conv_rows(x_ref, cw_ref, dst, post, g, r0):
        cols = slice(g * d, (g + 1) * d)
        w = cw_ref[:, cols]
        if r0 == 0:
            xpad[0:8, :] = jnp.zeros((8, d), F32)
            xpad[8:8 + rb, :] = x_ref[0:rb, cols]
        acc = None
        for j in range(CONV_K):
            if r0 == 0:
                xs = xpad[8 - halo + j:8 - halo + j + rb, :]
            else:
                xs = x_ref[r0 - halo + j:r0 - halo + j + rb, cols]
            t = xs * w[j:j + 1, :]
            acc = t if acc is None else acc + t
        dst[g, r0:r0 + rb, :] = post(_silu(acc))

    l2n = lambda y: y * lax.rsqrt(jnp.sum(y * y, -1, keepdims=True) + 1e-6)
    streams = ((q_ref, cwq_ref, qs, lambda y: l2n(y) * (d ** -0.5)),
               (k_ref, cwk_ref, ks, l2n),
               (v_ref, cwv_ref, vs, lambda y: y))
    prologue = [(r0, functools.partial(conv_rows, *strm, g, r0))
                for r0 in range(0, seq, rb) for g in range(heads) for strm in streams]

    for g in range(heads):
        a = ga_ref[0, g] + dtb_ref[g]
        sp = jnp.maximum(a, 0.0) + jnp.log1p(jnp.exp(-jnp.abs(a)))
        gl = jnp.where(lane < c_, -jnp.exp(alog_ref[g]) * sp, 0.0)
        sh = 1
        while sh < c_:
            gl = gl + jnp.where(lane >= sh, pltpu.roll(gl, sh, 1), 0.0)
            sh *= 2
        gcs[g] = gl
        bts[g] = jax.nn.sigmoid(gb_ref[0, g])

    for r0, task in prologue:
        if r0 < block_rows:
            task()

    def phase1(cg, between_levels):
        chains = [(cg * unroll1 + j, g) for j in range(unroll1) for g in range(heads)]
        st = []
        for c, g in chains:
            r0 = aligned(c * c_, c_)
            qc = qs[g, pl.ds(r0, c_), :]
            kc = ks[g, pl.ds(r0, c_), :]
            vc = vs[g, pl.ds(r0, c_), :]
            gcr = gcs[g, pl.ds(c, 1), :][:, :c_]
            btr = bts[g, pl.ds(c, 1), :][:, :c_]
            gcol = jnp.sum(jnp.where(eye, jnp.broadcast_to(gcr, (c_, c_)), 0.0), 1, keepdims=True)
            bcol = jnp.sum(jnp.where(eye, jnp.broadcast_to(btr, (c_, c_)), 0.0), 1, keepdims=True)
            glast = gcr[:, c_ - 1:c_]
            decay = jnp.where(tril, jnp.exp(jnp.where(tril, gcol - gcr, 0.0)), 0.0)
            kb = kc * bcol
            r = jnp.concatenate([vc * bcol, kb * jnp.exp(gcol)], axis=1)
            st.append(dict(c=c, g=g, r0=r0, qc=qc, kc=kc, kb=kb, gcol=gcol, glast=glast, decay=decay, r=r))
        for t in st:
            t["aq"] = _dot_nt(jnp.concatenate([t["kb"], t["qc"]], axis=0), t["kc"])
        between_levels()
        for t in st:
            t["nl"] = jnp.where(strict, -(t["aq"][:c_] * t["decay"]), 0.0)
            t["intra"] = jnp.where(tril, t["aq"][c_:] * t["decay"], 0.0)
        n_levels = c_.bit_length() - 1
        for t in st:
            nt0 = jnp.concatenate([t["nl"], jnp.zeros((c_, c_), F32)], axis=1) + eye_r
            t["nt"] = _dot(t["nl"], nt0) + eye_r
        between_levels()
        for lvl in range(1, n_levels):
            for t in st:
                nt = t["nt"]
                t["nt"] = _dot(nt[:, :c_], nt) + jnp.where(right, nt, 0.0)
            between_levels()
        for t in st:
            t["r"] = _dot(t["nt"], jnp.concatenate([jnp.zeros_like(t["r"]), t["r"]], axis=0))
        between_levels()
        for t in st:
            kd = t["kc"] * jnp.exp(t["glast"] - t["gcol"])
            t["kq"] = _dot_tn(kd, t["r"])
            t["io"] = _dot(t["intra"], t["r"])
        between_levels()
        for t in st:
            c, g, r0 = t["c"], t["g"], t["r0"]
            rd = aligned(c * d, d)
            q2s[g, pl.ds(rd, d), :] = t["kq"][:, :d]
            k2s[g, pl.ds(rd, d), :] = t["kq"][:, d:].astype(BF16)
            o0s[g, pl.ds(r0, c_), :] = t["io"][:, :d]
            rs[g, pl.ds(r0, c_), :] = (t["qc"] * jnp.exp(t["gcol"]) - t["io"][:, d:]).astype(BF16)
            egs[g, pl.ds(c, 1), :] = jnp.broadcast_to(jnp.exp(t["glast"]), (1, LANES))

    def phase2(c, states):
        r0 = aligned(c * c_, c_)
        rd = aligned(c * d, d)
        sbs = [states[g].astype(BF16) for g in range(heads)]
        ks2 = [_dot(k2s[g, pl.ds(rd, d), :], sbs[g]) for g in range(heads)]
        ros = [_dot(rs[g, pl.ds(r0, c_), :], sbs[g]) for g in range(heads)]
        new_states = [states[g] * egs[g, pl.ds(c, 1), :] + q2s[g, pl.ds(rd, d), :] - ks2[g]
                      for g in range(heads)]
        for g in range(heads):
            o0s[g, pl.ds(r0, c_), :] += ros[g]
        return tuple(new_states)

    def norm_rows(g, r0):
        cols = slice(g * d, (g + 1) * d)
        o = o0s[g, r0:r0 + rb, :]
        on = o * lax.rsqrt(jnp.mean(o * o, -1, keepdims=True) + 1e-6) * nw_ref[...]
        o_ref[r0:r0 + rb, cols] = (on * _silu(z_ref[r0:r0 + rb, cols])).astype(o_ref.dtype)

    epilogue = [(r0, functools.partial(norm_rows, g, r0)) for r0 in range(0, seq, rb) for g in range(heads)]

    n_hooks = c_.bit_length() + 3
    side_cast = [functools.partial(_cast_block, side_ref, side_out_ref, c0, min(c0 + SIDE_COLS, side_ref.shape[1]))
                 for c0 in range(0, side_ref.shape[1], SIDE_COLS)]
    hook, flush = drip([t for r0, t in prologue if r0 >= block_rows] + side_cast, n_hooks)
    phase1(0, hook)
    flush()

    per_hook = -(-unroll1 // n_hooks)

    def pipelined(blk, states):
        box, done = [states], [0]

        def recurrence_steps():
            for _ in range(per_hook):
                if done[0] < unroll1:
                    box[0] = phase2((blk - 1) * unroll1 + done[0], box[0])
                    done[0] += 1

        phase1(blk, recurrence_steps)
        while done[0] < unroll1:
            recurrence_steps()
        return box[0]

    states = lax.fori_loop(1, n_blocks, pipelined, tuple(jnp.zeros((d, d), F32) for _ in range(heads)))

    last0 = (n_blocks - 1) * unroll1
    hook, flush = drip([t for r0, t in epilogue if r0 < last0 * c_], nc - last0)
    for c in range(last0, nc):
        states = phase2(c, states)
        hook()
    flush()
    for r0, task in epilogue:
        if r0 >= last0 * c_:
            task()


def _gdn(proj, conv_w, ga_rows, gb_rows, a_log, dt_bias, norm_w, side, *, batch, seq, n_heads, heads):
    d = GDN_HEAD_DIM
    gw = heads * d
    hb = n_heads // heads
    nc = seq // CHUNK
    m = batch * seq
    kern = functools.partial(_gdn_kernel, heads=heads, seq=seq, unroll1=math.gcd(nc, 16), unroll2=2)
    col = lambda off: (lambda b, h: (b, off * hb + h))
    cw = lambda off: (lambda b, h: (0, off * hb + h))
    steps = batch * hb
    srows_all, scols = side.shape
    srows = -(-srows_all // (steps * 16)) * 16
    assert srows_all % 16 == 0 and srows <= srows_all
    slab = pl.BlockSpec((pl.Element(srows), pl.Element(scols)),
                        lambda b, h: (jnp.minimum((b * hb + h) * (srows // 16), (srows_all - srows) // 16) * 16, 0))
    return pl.pallas_call(
        kern,
        grid=(batch, hb),
        in_specs=[
            pl.BlockSpec((seq, gw), col(0)), pl.BlockSpec((seq, gw), col(1)),
            pl.BlockSpec((seq, gw), col(2)), pl.BlockSpec((seq, gw), col(3)),
            pl.BlockSpec((CONV_K, gw), cw(0)), pl.BlockSpec((CONV_K, gw), cw(1)),
            pl.BlockSpec((CONV_K, gw), cw(2)),
            pl.BlockSpec((1, heads, nc, LANES), lambda b, h: (b, h, 0, 0)),
            pl.BlockSpec((1, heads, nc, LANES), lambda b, h: (b, h, 0, 0)),
            pl.BlockSpec((heads, 1, 1), lambda b, h: (h, 0, 0)),
            pl.BlockSpec((heads, 1, 1), lambda b, h: (h, 0, 0)),
            pl.BlockSpec((1, d), lambda b, h: (0, 0)),
            slab,
        ],
        out_specs=[pl.BlockSpec((seq, gw), lambda b, h: (b, h)), slab],
        out_shape=[jax.ShapeDtypeStruct((m, n_heads * d), BF16), jax.ShapeDtypeStruct(side.shape, BF16)],
        scratch_shapes=[
            pltpu.VMEM((PROLOGUE_ROWS + 8, d), F32),
            pltpu.VMEM((heads, seq, d), F32),
            pltpu.VMEM((heads, seq, d), F32),
            pltpu.VMEM((heads, seq, d), F32),
            pltpu.VMEM((heads, nc * d, d), F32),
            pltpu.VMEM((heads, nc * d, d), BF16),
            pltpu.VMEM((heads, seq, d), F32),
            pltpu.VMEM((heads, seq, d), BF16),
            pltpu.VMEM((heads, nc, LANES), F32),
            pltpu.VMEM((heads, nc, LANES), F32),
            pltpu.VMEM((heads, nc, LANES), F32),
        ],
        compiler_params=pltpu.CompilerParams(
            dimension_semantics=("arbitrary", "arbitrary"), vmem_limit_bytes=VMEM_LIMIT),
        name="gdn",
    )(proj, proj, proj, proj, conv_w, conv_w, conv_w, ga_rows, gb_rows,
      a_log.reshape(n_heads, 1, 1), dt_bias.reshape(n_heads, 1, 1), norm_w.reshape(1, d), side)


def _diff_attn_kernel(q_ref, k_ref, vt_ref, lq1_ref, lk1_ref, lq2_ref, lk2_ref, nw_ref, sa_ref, sb_ref,
                      o_ref, sa_out_ref, sb_out_ref, acc1, acc2, *, tq, tk, seq, lambda_init):
    dh = DIFF_HEAD_DIM
    c2 = (dh ** -0.5) * math.log2(math.e)
    neg = float(jnp.finfo(jnp.float32).min)
    krow = lax.broadcasted_iota(jnp.int32, (tk, tq), 0)
    qcol = lax.broadcasted_iota(jnp.int32, (tk, tq), 1)
    lam = (jnp.exp(jnp.sum(lq1_ref[...] * lk1_ref[...], -1, keepdims=True))
           - jnp.exp(jnp.sum(lq2_ref[...] * lk2_ref[...], -1, keepdims=True)) + lambda_init)
    nw = nw_ref[...] * (1.0 - lambda_init)

    nq = seq // tq
    side_cols = sa_ref.shape[1]
    piece = -(-side_cols // (nq * LANES)) * LANES
    for i in range(nq):
        for src, dst in ((sa_ref, sa_out_ref), (sb_ref, sb_out_ref)):
            if i * piece < side_cols:
                _cast_block(src, dst, i * piece, min((i + 1) * piece, side_cols))
        q1 = q_ref[i * tq:(i + 1) * tq, :dh]
        q2 = q_ref[i * tq:(i + 1) * tq, dh:]
        m1 = m2 = jnp.full((1, tq), neg, F32)
        l1 = l2 = jnp.zeros((1, tq), F32)
        for j in range((i + 1) * tq // tk):
            kj = k_ref[j * tk:(j + 1) * tk, :]
            vt = vt_ref[:, j * tk:(j + 1) * tk]
            s1 = _dot_nt(kj[:, :dh], q1)
            s2 = _dot_nt(kj[:, dh:], q2)
            if (j + 1) * tk - 1 > i * tq:
                keep = krow + j * tk <= qcol + i * tq
                s1 = jnp.where(keep, s1, neg)
                s2 = jnp.where(keep, s2, neg)
            m1n = jnp.maximum(m1, jnp.max(s1, 0, keepdims=True))
            a1 = jnp.exp2((m1 - m1n) * c2)
            p1 = jnp.exp2((s1 - m1n) * c2)
            l1 = a1 * l1 + jnp.sum(p1, 0, keepdims=True)
            pv1 = _dot(vt, p1)
            m2n = jnp.maximum(m2, jnp.max(s2, 0, keepdims=True))
            a2 = jnp.exp2((m2 - m2n) * c2)
            p2 = jnp.exp2((s2 - m2n) * c2)
            l2 = a2 * l2 + jnp.sum(p2, 0, keepdims=True)
            pv2 = _dot(vt, p2)
            if j == 0:
                acc1[...] = pv1
                acc2[...] = pv2
            else:
                acc1[...] = acc1[...] * a1 + pv1
                acc2[...] = acc2[...] * a2 + pv2
            m1, m2 = m1n, m2n
        o = acc1[...] / l1 - lam * (acc2[...] / l2)
        on = o * lax.rsqrt(jnp.mean(o * o, 0, keepdims=True) + 1e-6) * nw
        o_ref[i * tq:(i + 1) * tq, :] = on.T.astype(o_ref.dtype)


def _diff_attn(qk, vt, lq1, lk1, lq2, lk2, norm_w, side_a, side_b, *, batch, seq, n_heads, tq, tk, lambda_init):
    dv = DIFF_V_DIM
    m = batch * seq
    steps = batch * n_heads
    assert side_a.shape == side_b.shape and side_a.shape[0] % (16 * steps) == 0
    srows, scols = side_a.shape[0] // steps, side_a.shape[1]
    slab = pl.BlockSpec((srows, scols), lambda b, h: (b * n_heads + h, 0))
    assert seq % tq == 0 and tq % tk == 0
    kern = functools.partial(_diff_attn_kernel, tq=tq, tk=tk, seq=seq, lambda_init=lambda_init)
    vec = lambda n: pl.BlockSpec((1, n), lambda b, h: (0, 0))
    return pl.pallas_call(
        kern,
        grid=(batch, n_heads),
        in_specs=[
            pl.BlockSpec((seq, dv), lambda b, h: (b, h)),
            pl.BlockSpec((seq, dv), lambda b, h: (b, n_heads + h)),
            pl.BlockSpec((dv, seq), lambda b, h: (h, b)),
            vec(DIFF_HEAD_DIM), vec(DIFF_HEAD_DIM), vec(DIFF_HEAD_DIM), vec(DIFF_HEAD_DIM),
            pl.BlockSpec((dv, 1), lambda b, h: (0, 0)),
            slab, slab,
        ],
        out_specs=[pl.BlockSpec((seq, dv), lambda b, h: (b, h)), slab, slab],
        out_shape=[jax.ShapeDtypeStruct((m, n_heads * dv), BF16),
                   jax.ShapeDtypeStruct(side_a.shape, BF16), jax.ShapeDtypeStruct(side_b.shape, BF16)],
        scratch_shapes=[pltpu.VMEM((dv, tq), F32), pltpu.VMEM((dv, tq), F32)],
        compiler_params=pltpu.CompilerParams(
            dimension_semantics=("parallel", "parallel"), vmem_limit_bytes=VMEM_LIMIT),
        name="diff_attn",
    )(qk, qk, vt, lq1.reshape(1, -1), lk1.reshape(1, -1), lq2.reshape(1, -1),
      lk2.reshape(1, -1), norm_w.reshape(-1, 1), side_a, side_b)


def _layer_norm(y, g, b):
    mu = jnp.mean(y, -1, keepdims=True)
    yc = y - mu
    var = jnp.mean(yc * yc, -1, keepdims=True)
    return yc * lax.rsqrt(var + 1e-5) * g + b


LN_ROWS = 64


def _deepnorm_epilogue(x_ref, acc_ref, g_ref, beta_ref, alpha, out_refs):
    g = g_ref[...]
    beta = beta_ref[...]

    def rows(i, carry):
        r0 = pl.multiple_of(i * LN_ROWS, LN_ROWS)
        sl = pl.ds(r0, LN_ROWS)
        y = _layer_norm(alpha * x_ref[sl, :] + acc_ref[sl, :], g, beta)
        for o in out_refs:
            o[sl, :] = y.astype(o.dtype)
        return carry

    lax.fori_loop(0, x_ref.shape[0] // LN_ROWS, rows, 0)


def _out_proj_kernel(a_ref, b_ref, w_ref, x_ref, g_ref, beta_ref, o_ref, ob_ref, *, nk, alpha):
    k = pl.program_id(1)
    half = nk // 2

    @pl.when(k == 0)
    def _():
        o_ref[...] = jnp.zeros_like(o_ref)

    def accumulate(lhs_ref):
        n = o_ref.shape[1]
        nb = min(n, 512)
        lhs = lhs_ref[...]
        for n0 in range(0, n, nb):
            o_ref[:, n0:n0 + nb] += jnp.dot(lhs, w_ref[:, n0:n0 + nb], preferred_element_type=F32)

    @pl.when(k < half)
    def _():
        accumulate(a_ref)

    @pl.when(k >= half)
    def _():
        accumulate(b_ref)

    @pl.when(k == nk - 1)
    def _():
        _deepnorm_epilogue(x_ref, o_ref, g_ref, beta_ref, alpha, (o_ref, ob_ref))


def _out_proj_ln(a, b, w, x, g, beta, *, bm, bk, alpha):
    m, ka = a.shape
    n = w.shape[1]
    half = ka // bk
    nk = 2 * half
    kern = functools.partial(_out_proj_kernel, nk=nk, alpha=alpha)
    return pl.pallas_call(
        kern,
        grid=(m // bm, nk),
        in_specs=[
            pl.BlockSpec((bm, bk), lambda i, k: (i, jnp.minimum(k, half - 1))),
            pl.BlockSpec((bm, bk), lambda i, k: (i, jnp.maximum(k - half, 0))),
            pl.BlockSpec((bk, n), lambda i, k: (k, 0)),
            pl.BlockSpec((bm, n), lambda i, k: (i, 0)),
            pl.BlockSpec((1, n), lambda i, k: (0, 0)),
            pl.BlockSpec((1, n), lambda i, k: (0, 0)),
        ],
        out_specs=[pl.BlockSpec((bm, n), lambda i, k: (i, 0)),
                   pl.BlockSpec((bm, n), lambda i, k: (i, 0))],
        out_shape=[jax.ShapeDtypeStruct((m, n), F32), jax.ShapeDtypeStruct((m, n), BF16)],
        compiler_params=pltpu.CompilerParams(
            dimension_semantics=("parallel", "arbitrary"), vmem_limit_bytes=VMEM_LIMIT),
        name="out_proj_ln",
    )(a, b, w, x, g.reshape(1, n), beta.reshape(1, n))


def _ffn_kernel(xb_ref, wg_ref, wu_ref, wd_ref, x_ref, g_ref, beta_ref, o_ref, *, nf, alpha):
    f = pl.program_id(1)

    @pl.when(f == 0)
    def _():
        o_ref[...] = jnp.zeros_like(o_ref)

    xb = xb_ref[...]
    gate = jnp.dot(xb, wg_ref[...], preferred_element_type=F32)
    up = jnp.dot(xb, wu_ref[...], preferred_element_type=F32)
    hid = (_silu(gate) * up).astype(BF16)
    dm = o_ref.shape[1]
    nb = min(dm, 512)
    for n0 in range(0, dm, nb):
        o_ref[:, n0:n0 + nb] += jnp.dot(hid, wd_ref[:, n0:n0 + nb], preferred_element_type=F32)

    @pl.when(f == nf - 1)
    def _():
        _deepnorm_epilogue(x_ref, o_ref, g_ref, beta_ref, alpha, (o_ref,))


def _ffn_ln(xb, x, wg, wu, wd, g, beta, *, bm, bf, alpha):
    m, dm = xb.shape
    ff = wg.shape[1]
    nf = ff // bf
    assert ff % bf == 0
    kern = functools.partial(_ffn_kernel, nf=nf, alpha=alpha)
    return pl.pallas_call(
        kern,
        grid=(m // bm, nf),
        in_specs=[
            pl.BlockSpec((bm, dm), lambda i, f: (i, 0)),
            pl.BlockSpec((dm, bf), lambda i, f: (0, f)),
            pl.BlockSpec((dm, bf), lambda i, f: (0, f)),
            pl.BlockSpec((bf, dm), lambda i, f: (f, 0)),
            pl.BlockSpec((bm, dm), lambda i, f: (i, 0)),
            pl.BlockSpec((1, dm), lambda i, f: (0, 0)),
            pl.BlockSpec((1, dm), lambda i, f: (0, 0)),
        ],
        out_specs=pl.BlockSpec((bm, dm), lambda i, f: (i, 0)),
        out_shape=jax.ShapeDtypeStruct((m, dm), F32),
        compiler_params=pltpu.CompilerParams(
            dimension_semantics=("parallel", "arbitrary"), vmem_limit_bytes=VMEM_LIMIT),
        name="ffn_ln",
    )(xb, wg, wu, wd, x, g.reshape(1, dm), beta.reshape(1, dm))


def _layer(x2, batch, seq, l, w_in, conv_w, a_log, dt_bias, gdn_norm_w, lq1, lk1, lq2, lk2,
           diff_norm_w, w_out, ln1_g, ln1_b, w_gate, w_up, w_down, ln2_g, ln2_b, alpha):
    m, dm = x2.shape
    n_gdn_heads = a_log.shape[0]
    gdn_w = n_gdn_heads * GDN_HEAD_DIM
    diff_w = w_out.shape[0] - gdn_w
    n_diff_heads = diff_w // DIFF_V_DIM
    lambda_init = 0.8 - 0.6 * math.exp(-0.3 * l)
    nc = seq // CHUNK

    bn = 768
    main = 4 * gdn_w
    ab = 2 * n_gdn_heads
    wide = -(-(main + ab) // bn) * bn
    w_t = w_in.T
    xb = x2.astype(BF16)

    proj_g = _matmul_wt32(xb, w_t, 0, wide, 1024, bn, F32, "proj_gdn")
    qk_w = 2 * n_diff_heads * 2 * DIFF_HEAD_DIM
    proj_qk = _matmul_wt32(xb, w_t, main + ab, qk_w, 1024, 512, BF16, "proj_qk")
    proj_vt, wo_b = _matmul_wt32(xb, w_t, main + ab + qk_w, diff_w, 1024, 512, BF16, "proj_vt",
                                 transposed_out=True, side=(w_out, 128))

    gates = proj_g[:, main:main + ab].reshape(batch, seq, ab).transpose(0, 2, 1)
    gates = gates.reshape(batch, ab, nc, CHUNK)
    gates = jnp.pad(gates, ((0, 0), (0, 0), (0, 0), (0, LANES - CHUNK)))
    gdn_out, wd_b = _gdn(proj_g, conv_w, gates[:, :n_gdn_heads], gates[:, n_gdn_heads:], a_log, dt_bias,
                         gdn_norm_w, w_down, batch=batch, seq=seq, n_heads=n_gdn_heads, heads=2)

    diff_out, wg_b, wu_b = _diff_attn(proj_qk, proj_vt, lq1, lk1, lq2, lk2, diff_norm_w, w_gate, w_up,
                                      batch=batch, seq=seq, n_heads=n_diff_heads, tq=512, tk=512,
                                      lambda_init=lambda_init)

    x1, x1b = _out_proj_ln(gdn_out, diff_out, wo_b, x2, ln1_g, ln1_b, bm=512, bk=512, alpha=alpha)
    return _ffn_ln(x1b, x1, wg_b, wu_b, wd_b, ln2_g, ln2_b, bm=512, bf=256, alpha=alpha)


def kernel(x, w_in, conv_w, a_log, dt_bias, gdn_norm_w, lambda_q1, lambda_k1, lambda_q2, lambda_k2,
           diff_norm_w, w_out, ln1_g, ln1_b, w_gate, w_up, w_down, ln2_g, ln2_b):
    batch, seq, dm = x.shape
    depth = w_in.shape[0]
    alpha = (2 * depth) ** 0.25
    x2 = x.reshape(batch * seq, dm)
    for l in range(depth):
        x2 = _layer(x2, batch, seq, l, w_in[l], conv_w[l], a_log[l], dt_bias[l], gdn_norm_w[l],
                    lambda_q1[l], lambda_k1[l], lambda_q2[l], lambda_k2[l], diff_norm_w[l], w_out[l],
                    ln1_g[l], ln1_b[l], w_gate[l], w_up[l], w_down[l], ln2_g[l], ln2_b[l], alpha)
    return x2.reshape(batch, seq, dm)
```

```python
import functools
import math

import jax
import jax.numpy as jnp
from jax import lax
from jax.experimental import pallas as pl
from jax.experimental.pallas import tpu as pltpu

F32 = jnp.float32
BF16 = jnp.bfloat16

GDN_HEAD_DIM = 128
DIFF_HEAD_DIM = 128
DIFF_V_DIM = 256
CONV_K = 4
CHUNK = 64
PROLOGUE_ROWS = 128
LANES = 128
VMEM_LIMIT = 60 * 1024 * 1024


def _dot(a, b):
    return jnp.dot(a.astype(BF16), b.astype(BF16), preferred_element_type=F32)


def _dot_nt(a, b):
    return lax.dot_general(a.astype(BF16), b.astype(BF16), (((1,), (1,)), ((), ())),
                           preferred_element_type=F32)


def _dot_tn(a, b):
    return lax.dot_general(a.astype(BF16), b.astype(BF16), (((0,), (0,)), ((), ())),
                           preferred_element_type=F32)


def _silu(x):
    return x * jax.nn.sigmoid(x)


CAST_ROWS = 128


SIDE_COLS = 1024


def _cast_block(src_ref, dst_ref, col0=0, col1=None):
    col1 = src_ref.shape[1] if col1 is None else col1
    for c0 in range(col0, col1, SIDE_COLS):
        c1 = min(c0 + SIDE_COLS, col1)
        dst_ref[:, c0:c1] = src_ref[:, c0:c1].astype(BF16)


def _mm_wt32_kernel(x_ref, w_ref, *rest, transposed_out, has_side):
    if has_side:
        side_ref, o_ref, side_out_ref, wb_ref = rest
        _cast_block(side_ref, side_out_ref)
    else:
        o_ref, wb_ref = rest

    @pl.when(pl.program_id(1) == 0)
    def _():
        for r in range(0, w_ref.shape[0], CAST_ROWS):
            wb_ref[r:r + CAST_ROWS, :] = w_ref[r:r + CAST_ROWS, :].astype(BF16)

    if transposed_out:
        o_ref[...] = _dot_nt(wb_ref[...], x_ref[...]).astype(o_ref.dtype)
    else:
        o_ref[...] = _dot_nt(x_ref[...], wb_ref[...]).astype(o_ref.dtype)


def _matmul_wt32(x, w_t, row0, n_out, bm, bn, out_dtype, name, transposed_out=False, side=None):
    m, k = x.shape
    assert m % bm == 0 and n_out % bn == 0 and bn % CAST_ROWS == 0 and row0 % 8 == 0
    assert row0 + n_out <= w_t.shape[0] and k == w_t.shape[1]
    nj, ni = n_out // bn, m // bm
    if transposed_out:
        out_specs = [pl.BlockSpec((bn, bm), lambda j, i: (j, i))]
        out_shape = [jax.ShapeDtypeStruct((n_out, m), out_dtype)]
    else:
        out_specs = [pl.BlockSpec((bm, bn), lambda j, i: (i, j))]
        out_shape = [jax.ShapeDtypeStruct((m, n_out), out_dtype)]
    in_specs = [pl.BlockSpec((bm, k), lambda j, i: (i, 0)),
                pl.BlockSpec((pl.Element(bn), pl.Element(k)), lambda j, i: ((row0 // 8 + j * (bn // 8)) * 8, 0))]
    args = [x, w_t]
    if side is not None:
        a, rows = side
        n_slabs = a.shape[0] // rows
        assert a.shape[0] % rows == 0 and rows % 16 == 0 and n_slabs <= nj * ni
        slab = lambda j, i: (jnp.minimum(j * ni + i, n_slabs - 1), 0)
        in_specs.append(pl.BlockSpec((rows, a.shape[1]), slab))
        out_specs.append(pl.BlockSpec((rows, a.shape[1]), slab))
        out_shape.append(jax.ShapeDtypeStruct(a.shape, BF16))
        args.append(a)
    outs = pl.pallas_call(
        functools.partial(_mm_wt32_kernel, transposed_out=transposed_out, has_side=side is not None),
        grid=(nj, ni),
        in_specs=in_specs,
        out_specs=out_specs,
        out_shape=out_shape,
        scratch_shapes=[pltpu.VMEM((bn, k), BF16)],
        compiler_params=pltpu.CompilerParams(
            dimension_semantics=("arbitrary", "arbitrary"), vmem_limit_bytes=VMEM_LIMIT),
        name=name,
    )(*args)
    return outs if side is not None else outs[0]


def _gdn_kernel(q_ref, k_ref, v_ref, z_ref, cwq_ref, cwk_ref, cwv_ref, ga_ref, gb_ref,
                alog_ref, dtb_ref, nw_ref, o_ref,
                xpad, qs, ks, vs, q2s, k2s, o0s, rs, gcs, bts, egs, *, heads, seq, unroll1, unroll2):
    d = GDN_HEAD_DIM
    c_ = CHUNK
    nc = seq // c_
    row = lax.broadcasted_iota(jnp.int32, (c_, c_), 0)
    col = lax.broadcasted_iota(jnp.int32, (c_, c_), 1)
    tril = col <= row
    strict = col < row
    eye = col == row
    col2 = lax.broadcasted_iota(jnp.int32, (c_, 2 * c_), 1)
    right = col2 >= c_
    eye_r = jnp.where(col2 == lax.broadcasted_iota(jnp.int32, (c_, 2 * c_), 0) + c_, 1.0, 0.0)
    lane = lax.broadcasted_iota(jnp.int32, (nc, LANES), 1)

    rb = min(seq, PROLOGUE_ROWS)
    halo = CONV_K - 1
    n_blocks = nc // unroll1
    block_rows = unroll1 * c_

    def aligned(x, m):
        return x if isinstance(x, int) else pl.multiple_of(x, m)

    def drip(tasks, n_calls):
        it = iter(tasks)
        per = -(-len(tasks) // max(n_calls, 1))

        def hook():
            for _ in range(per):
                t = next(it, None)
                if t is not None:
                    t()

        def flush():
            for t in it:
                t()

        return hook, flush

    def conv_rows(x_ref, cw_ref, dst, post, g, r0):
        cols = slice(g * d, (g + 1) * d)
        w = cw_ref[:, cols]
        if r0 == 0:
            xpad[0:8, :] = jnp.zeros((8, d), F32)
            xpad[8:8 + rb, :] = x_ref[0:rb, cols]
        acc = None
        for j in range(CONV_K):
            if r0 == 0:
                xs = xpad[8 - halo + j:8 - halo + j + rb, :]
            else:
                xs = x_ref[r0 - halo + j:r0 - halo + j + rb, cols]
            t = xs * w[j:j + 1, :]
            acc = t if acc is None else acc + t
        dst[g, r0:r0 + rb, :] = post(_silu(acc))

    l2n = lambda y: y * lax.rsqrt(jnp.sum(y * y, -1, keepdims=True) + 1e-6)
    streams = ((q_ref, cwq_ref, qs, lambda y: l2n(y) * (d ** -0.5)),
               (k_ref, cwk_ref, ks, l2n),
               (v_ref, cwv_ref, vs, lambda y: y))
    prologue = [(r0, functools.partial(conv_rows, *strm, g, r0))
                for r0 in range(0, seq, rb) for g in range(heads) for strm in streams]

    for g in range(heads):
        a = ga_ref[0, g] + dtb_ref[g]
        sp = jnp.maximum(a, 0.0) + jnp.log1p(jnp.exp(-jnp.abs(a)))
        gl = jnp.where(lane < c_, -jnp.exp(alog_ref[g]) * sp, 0.0)
        sh = 1
        while sh < c_:
            gl = gl + jnp.where(lane >= sh, pltpu.roll(gl, sh, 1), 0.0)
            sh *= 2
        gcs[g] = gl
        bts[g] = jax.nn.sigmoid(gb_ref[0, g])

    for r0, task in prologue:
        if r0 < block_rows:
            task()

    def phase1(cg, between_levels):
        chains = [(cg * unroll1 + j, g) for j in range(unroll1) for g in range(heads)]
        st = []
        for c, g in chains:
            r0 = aligned(c * c_, c_)
            qc = qs[g, pl.ds(r0, c_), :]
            kc = ks[g, pl.ds(r0, c_), :]
            vc = vs[g, pl.ds(r0, c_), :]
            gcr = gcs[g, pl.ds(c, 1), :][:, :c_]
            btr = bts[g, pl.ds(c, 1), :][:, :c_]
            gcol = jnp.sum(jnp.where(eye, jnp.broadcast_to(gcr, (c_, c_)), 0.0), 1, keepdims=True)
            bcol = jnp.sum(jnp.where(eye, jnp.broadcast_to(btr, (c_, c_)), 0.0), 1, keepdims=True)
            glast = gcr[:, c_ - 1:c_]
            decay = jnp.where(tril, jnp.exp(jnp.where(tril, gcol - gcr, 0.0)), 0.0)
            kb = kc * bcol
            r = jnp.concatenate([vc * bcol, kb * jnp.exp(gcol)], axis=1)
            st.append(dict(c=c, g=g, r0=r0, qc=qc, kc=kc, kb=kb, gcol=gcol, glast=glast, decay=decay, r=r))
        for t in st:
            t["aq"] = _dot_nt(jnp.concatenate([t["kb"], t["qc"]], axis=0), t["kc"])
        between_levels()
        for t in st:
            t["nl"] = jnp.where(strict, -(t["aq"][:c_] * t["decay"]), 0.0)
            t["intra"] = jnp.where(tril, t["aq"][c_:] * t["decay"], 0.0)
        n_levels = c_.bit_length() - 1
        for t in st:
            nt0 = jnp.concatenate([t["nl"], jnp.zeros((c_, c_), F32)], axis=1) + eye_r
            t["nt"] = _dot(t["nl"], nt0) + eye_r
        between_levels()
        for lvl in range(1, n_levels):
            for t in st:
                nt = t["nt"]
                t["nt"] = _dot(nt[:, :c_], nt) + jnp.where(right, nt, 0.0)
            between_levels()
        for t in st:
            t["r"] = _dot(t["nt"], jnp.concatenate([jnp.zeros_like(t["r"]), t["r"]], axis=0))
        between_levels()
        for t in st:
            kd = t["kc"] * jnp.exp(t["glast"] - t["gcol"])
            t["kq"] = _dot_tn(kd, t["r"])
            t["io"] = _dot(t["intra"], t["r"])
        between_levels()
        for t in st:
            c, g, r0 = t["c"], t["g"], t["r0"]
            rd = aligned(c * d, d)
            q2s[g, pl.ds(rd, d), :] = t["kq"][:, :d]
            k2s[g, pl.ds(rd, d), :] = t["kq"][:, d:].astype(BF16)
            o0s[g, pl.ds(r0, c_), :] = t["io"][:, :d]
            rs[g, pl.ds(r0, c_), :] = (t["qc"] * jnp.exp(t["gcol"]) - t["io"][:, d:]).astype(BF16)
            egs[g, pl.ds(c, 1), :] = jnp.broadcast_to(jnp.exp(t["glast"]), (1, LANES))

    def phase2(c, states):
        r0 = aligned(c * c_, c_)
        rd = aligned(c * d, d)
        sbs = [states[g].astype(BF16) for g in range(heads)]
        ks2 = [_dot(k2s[g, pl.ds(rd, d), :], sbs[g]) for g in range(heads)]
        ros = [_dot(rs[g, pl.ds(r0, c_), :], sbs[g]) for g in range(heads)]
        new_states = [states[g] * egs[g, pl.ds(c, 1), :] + q2s[g, pl.ds(rd, d), :] - ks2[g]
                      for g in range(heads)]
        for g in range(heads):
            o0s[g, pl.ds(r0, c_), :] += ros[g]
        return tuple(new_states)

    def norm_rows(g, r0):
        cols = slice(g * d, (g + 1) * d)
        o = o0s[g, r0:r0 + rb, :]
        on = o * lax.rsqrt(jnp.mean(o * o, -1, keepdims=True) + 1e-6) * nw_ref[...]
        o_ref[r0:r0 + rb, cols] = (on * _silu(z_ref[r0:r0 + rb, cols])).astype(o_ref.dtype)

    epilogue = [(r0, functools.partial(norm_rows, g, r0)) for r0 in range(0, seq, rb) for g in range(heads)]

    n_hooks = c_.bit_length() + 3
    hook, flush = drip([t for r0, t in prologue if r0 >= block_rows], n_hooks)
    phase1(0, hook)
    flush()

    per_hook = -(-unroll1 // n_hooks)

    def pipelined(blk, states):
        box, done = [states], [0]

        def recurrence_steps():
            for _ in range(per_hook):
                if done[0] < unroll1:
                    box[0] = phase2((blk - 1) * unroll1 + done[0], box[0])
                    done[0] += 1

        phase1(blk, recurrence_steps)
        while done[0] < unroll1:
            recurrence_steps()
        return box[0]

    states = lax.fori_loop(1, n_blocks, pipelined, tuple(jnp.zeros((d, d), F32) for _ in range(heads)))

    last0 = (n_blocks - 1) * unroll1
    hook, flush = drip([t for r0, t in epilogue if r0 < last0 * c_], nc - last0)
    for c in range(last0, nc):
        states = phase2(c, states)
        hook()
    flush()
    for r0, task in epilogue:
        if r0 >= last0 * c_:
            task()


def _gdn(proj, conv_w, ga_rows, gb_rows, a_log, dt_bias, norm_w, *, batch, seq, n_heads, heads):
    d = GDN_HEAD_DIM
    gw = heads * d
    hb = n_heads // heads
    nc = seq // CHUNK
    m = batch * seq
    kern = functools.partial(_gdn_kernel, heads=heads, seq=seq, unroll1=math.gcd(nc, 16), unroll2=2)
    col = lambda off: (lambda b, h: (b, off * hb + h))
    cw = lambda off: (lambda b, h: (0, off * hb + h))
    return pl.pallas_call(
        kern,
        grid=(batch, hb),
        in_specs=[
            pl.BlockSpec((seq, gw), col(0)), pl.BlockSpec((seq, gw), col(1)),
            pl.BlockSpec((seq, gw), col(2)), pl.BlockSpec((seq, gw), col(3)),
            pl.BlockSpec((CONV_K, gw), cw(0)), pl.BlockSpec((CONV_K, gw), cw(1)),
            pl.BlockSpec((CONV_K, gw), cw(2)),
            pl.BlockSpec((1, heads, nc, LANES), lambda b, h: (b, h, 0, 0)),
            pl.BlockSpec((1, heads, nc, LANES), lambda b, h: (b, h, 0, 0)),
            pl.BlockSpec((heads, 1, 1), lambda b, h: (h, 0, 0)),
            pl.BlockSpec((heads, 1, 1), lambda b, h: (h, 0, 0)),
            pl.BlockSpec((1, d), lambda b, h: (0, 0)),
        ],
        out_specs=pl.BlockSpec((seq, gw), lambda b, h: (b, h)),
        out_shape=jax.ShapeDtypeStruct((m, n_heads * d), BF16),
        scratch_shapes=[
            pltpu.VMEM((PROLOGUE_ROWS + 8, d), F32),
            pltpu.VMEM((heads, seq, d), F32),
            pltpu.VMEM((heads, seq, d), F32),
            pltpu.VMEM((heads, seq, d), F32),
            pltpu.VMEM((heads, nc * d, d), F32),
            pltpu.VMEM((heads, nc * d, d), BF16),
            pltpu.VMEM((heads, seq, d), F32),
            pltpu.VMEM((heads, seq, d), BF16),
            pltpu.VMEM((heads, nc, LANES), F32),
            pltpu.VMEM((heads, nc, LANES), F32),
            pltpu.VMEM((heads, nc, LANES), F32),
        ],
        compiler_params=pltpu.CompilerParams(
            dimension_semantics=("parallel", "parallel"), vmem_limit_bytes=VMEM_LIMIT),
        name="gdn",
    )(proj, proj, proj, proj, conv_w, conv_w, conv_w, ga_rows, gb_rows,
      a_log.reshape(n_heads, 1, 1), dt_bias.reshape(n_heads, 1, 1), norm_w.reshape(1, d))


def _diff_attn_kernel(q_ref, k_ref, vt_ref, lq1_ref, lk1_ref, lq2_ref, lk2_ref, nw_ref, sa_ref, sb_ref,
                      o_ref, sa_out_ref, sb_out_ref, acc1, acc2, *, tq, tk, seq, lambda_init):
    dh = DIFF_HEAD_DIM
    c2 = (dh ** -0.5) * math.log2(math.e)
    neg = float(jnp.finfo(jnp.float32).min)
    krow = lax.broadcasted_iota(jnp.int32, (tk, tq), 0)
    qcol = lax.broadcasted_iota(jnp.int32, (tk, tq), 1)
    lam = (jnp.exp(jnp.sum(lq1_ref[...] * lk1_ref[...], -1, keepdims=True))
           - jnp.exp(jnp.sum(lq2_ref[...] * lk2_ref[...], -1, keepdims=True)) + lambda_init)
    nw = nw_ref[...] * (1.0 - lambda_init)

    nq = seq // tq
    side_cols = sa_ref.shape[1]
    piece = -(-side_cols // (nq * LANES)) * LANES
    for i in range(nq):
        for src, dst in ((sa_ref, sa_out_ref), (sb_ref, sb_out_ref)):
            if i * piece < side_cols:
                _cast_block(src, dst, i * piece, min((i + 1) * piece, side_cols))
        q1 = q_ref[i * tq:(i + 1) * tq, :dh]
        q2 = q_ref[i * tq:(i + 1) * tq, dh:]
        m1 = m2 = jnp.full((1, tq), neg, F32)
        l1 = l2 = jnp.zeros((1, tq), F32)
        for j in range((i + 1) * tq // tk):
            kj = k_ref[j * tk:(j + 1) * tk, :]
            vt = vt_ref[:, j * tk:(j + 1) * tk]
            s1 = _dot_nt(kj[:, :dh], q1)
            s2 = _dot_nt(kj[:, dh:], q2)
            if (j + 1) * tk - 1 > i * tq:
                keep = krow + j * tk <= qcol + i * tq
                s1 = jnp.where(keep, s1, neg)
                s2 = jnp.where(keep, s2, neg)
            m1n = jnp.maximum(m1, jnp.max(s1, 0, keepdims=True))
            a1 = jnp.exp2((m1 - m1n) * c2)
            p1 = jnp.exp2((s1 - m1n) * c2)
            l1 = a1 * l1 + jnp.sum(p1, 0, keepdims=True)
            pv1 = _dot(vt, p1)
            m2n = jnp.maximum(m2, jnp.max(s2, 0, keepdims=True))
            a2 = jnp.exp2((m2 - m2n) * c2)
            p2 = jnp.exp2((s2 - m2n) * c2)
            l2 = a2 * l2 + jnp.sum(p2, 0, keepdims=True)
            pv2 = _dot(vt, p2)
            if j == 0:
                acc1[...] = pv1
                acc2[...] = pv2
            else:
                acc1[...] = acc1[...] * a1 + pv1
                acc2[...] = acc2[...] * a2 + pv2
            m1, m2 = m1n, m2n
        o = acc1[...] / l1 - lam * (acc2[...] / l2)
        on = o * lax.rsqrt(jnp.mean(o * o, 0, keepdims=True) + 1e-6) * nw
        o_ref[i * tq:(i + 1) * tq, :] = on.T.astype(o_ref.dtype)


def _diff_attn(qk, vt, lq1, lk1, lq2, lk2, norm_w, side_a, side_b, *, batch, seq, n_heads, tq, tk, lambda_init):
    dv = DIFF_V_DIM
    m = batch * seq
    steps = batch * n_heads
    assert side_a.shape == side_b.shape and side_a.shape[0] % (16 * steps) == 0
    srows, scols = side_a.shape[0] // steps, side_a.shape[1]
    slab = pl.BlockSpec((srows, scols), lambda b, h: (b * n_heads + h, 0))
    assert seq % tq == 0 and tq % tk == 0
    kern = functools.partial(_diff_attn_kernel, tq=tq, tk=tk, seq=seq, lambda_init=lambda_init)
    vec = lambda n: pl.BlockSpec((1, n), lambda b, h: (0, 0))
    return pl.pallas_call(
        kern,
        grid=(batch, n_heads),
        in_specs=[
            pl.BlockSpec((seq, dv), lambda b, h: (b, h)),
            pl.BlockSpec((seq, dv), lambda b, h: (b, n_heads + h)),
            pl.BlockSpec((dv, seq), lambda b, h: (h, b)),
            vec(DIFF_HEAD_DIM), vec(DIFF_HEAD_DIM), vec(DIFF_HEAD_DIM), vec(DIFF_HEAD_DIM),
            pl.BlockSpec((dv, 1), lambda b, h: (0, 0)),
            slab, slab,
        ],
        out_specs=[pl.BlockSpec((seq, dv), lambda b, h: (b, h)), slab, slab],
        out_shape=[jax.ShapeDtypeStruct((m, n_heads * dv), BF16),
                   jax.ShapeDtypeStruct(side_a.shape, BF16), jax.ShapeDtypeStruct(side_b.shape, BF16)],
        scratch_shapes=[pltpu.VMEM((dv, tq), F32), pltpu.VMEM((dv, tq), F32)],
        compiler_params=pltpu.CompilerParams(
            dimension_semantics=("parallel", "parallel"), vmem_limit_bytes=VMEM_LIMIT),
        name="diff_attn",
    )(qk, qk, vt, lq1.reshape(1, -1), lk1.reshape(1, -1), lq2.reshape(1, -1),
      lk2.reshape(1, -1), norm_w.reshape(-1, 1), side_a, side_b)


def _layer_norm(y, g, b):
    mu = jnp.mean(y, -1, keepdims=True)
    yc = y - mu
    var = jnp.mean(yc * yc, -1, keepdims=True)
    return yc * lax.rsqrt(var + 1e-5) * g + b


LN_ROWS = 64


def _deepnorm_epilogue(x_ref, acc_ref, g_ref, beta_ref, alpha, out_refs):
    g = g_ref[...]
    beta = beta_ref[...]

    def rows(i, carry):
        r0 = pl.multiple_of(i * LN_ROWS, LN_ROWS)
        sl = pl.ds(r0, LN_ROWS)
        y = _layer_norm(alpha * x_ref[sl, :] + acc_ref[sl, :], g, beta)
        for o in out_refs:
            o[sl, :] = y.astype(o.dtype)
        return carry

    lax.fori_loop(0, x_ref.shape[0] // LN_ROWS, rows, 0)


def _out_proj_kernel(a_ref, b_ref, w_ref, x_ref, g_ref, beta_ref, o_ref, *, nk, alpha):
    k = pl.program_id(1)
    half = nk // 2

    def accumulate(lhs_ref, first):
        n = o_ref.shape[1]
        nb = min(n, 512)
        lhs = lhs_ref[...]
        for n0 in range(0, n, nb):
            part = jnp.dot(lhs, w_ref[:, n0:n0 + nb], preferred_element_type=F32)
            if first:
                o_ref[:, n0:n0 + nb] = part
            else:
                o_ref[:, n0:n0 + nb] += part

    @pl.when(k == 0)
    def _():
        accumulate(a_ref, True)

    @pl.when(jnp.logical_and(k > 0, k < half))
    def _():
        accumulate(a_ref, False)

    @pl.when(k >= half)
    def _():
        accumulate(b_ref, False)

    @pl.when(k == nk - 1)
    def _():
        _deepnorm_epilogue(x_ref, o_ref, g_ref, beta_ref, alpha, (o_ref,))


def _out_proj_ln(a, b, w, x, g, beta, *, bm, bk, alpha):
    m, ka = a.shape
    n = w.shape[1]
    half = ka // bk
    nk = 2 * half
    kern = functools.partial(_out_proj_kernel, nk=nk, alpha=alpha)
    return pl.pallas_call(
        kern,
        grid=(m // bm, nk),
        in_specs=[
            pl.BlockSpec((bm, bk), lambda i, k: (i, jnp.minimum(k, half - 1))),
            pl.BlockSpec((bm, bk), lambda i, k: (i, jnp.maximum(k - half, 0))),
            pl.BlockSpec((bk, n), lambda i, k: (k, 0)),
            pl.BlockSpec((bm, n), lambda i, k: (i, 0)),
            pl.BlockSpec((1, n), lambda i, k: (0, 0)),
            pl.BlockSpec((1, n), lambda i, k: (0, 0)),
        ],
        out_specs=pl.BlockSpec((bm, n), lambda i, k: (i, 0)),
        out_shape=jax.ShapeDtypeStruct((m, n), F32),
        compiler_params=pltpu.CompilerParams(
            dimension_semantics=("parallel", "arbitrary"), vmem_limit_bytes=VMEM_LIMIT),
        name="out_proj_ln",
    )(a, b, w, x, g.reshape(1, n), beta.reshape(1, n))


def _ffn_kernel(wg_ref, wu_ref, wd_ref, x_ref, g_ref, beta_ref, o_ref, xb_ref, *, nf, alpha):
    f = pl.program_id(1)

    @pl.when(f == 0)
    def _():
        o_ref[...] = jnp.zeros_like(o_ref)
        for r in range(0, x_ref.shape[0], LN_ROWS):
            xb_ref[r:r + LN_ROWS, :] = x_ref[r:r + LN_ROWS, :].astype(BF16)

    xb = xb_ref[...]
    gate = jnp.dot(xb, wg_ref[...], preferred_element_type=F32)
    up = jnp.dot(xb, wu_ref[...], preferred_element_type=F32)
    hid = (_silu(gate) * up).astype(BF16)
    dm = o_ref.shape[1]
    nb = min(dm, 512)
    for n0 in range(0, dm, nb):
        o_ref[:, n0:n0 + nb] += jnp.dot(hid, wd_ref[:, n0:n0 + nb], preferred_element_type=F32)

    @pl.when(f == nf - 1)
    def _():
        _deepnorm_epilogue(x_ref, o_ref, g_ref, beta_ref, alpha, (o_ref,))


def _ffn_ln(x, wg, wu, wd, g, beta, *, bm, bf, alpha):
    m, dm = x.shape
    ff = wg.shape[1]
    nf = ff // bf
    assert ff % bf == 0
    kern = functools.partial(_ffn_kernel, nf=nf, alpha=alpha)
    return pl.pallas_call(
        kern,
        grid=(m // bm, nf),
        in_specs=[
            pl.BlockSpec((dm, bf), lambda i, f: (0, f)),
            pl.BlockSpec((dm, bf), lambda i, f: (0, f)),
            pl.BlockSpec((bf, dm), lambda i, f: (f, 0)),
            pl.BlockSpec((bm, dm), lambda i, f: (i, 0)),
            pl.BlockSpec((1, dm), lambda i, f: (0, 0)),
            pl.BlockSpec((1, dm), lambda i, f: (0, 0)),
        ],
        out_specs=pl.BlockSpec((bm, dm), lambda i, f: (i, 0)),
        out_shape=jax.ShapeDtypeStruct((m, dm), F32),
        scratch_shapes=[pltpu.VMEM((bm, dm), BF16)],
        compiler_params=pltpu.CompilerParams(
            dimension_semantics=("parallel", "arbitrary"), vmem_limit_bytes=VMEM_LIMIT),
        name="ffn_ln",
    )(wg, wu, wd, x, g.reshape(1, dm), beta.reshape(1, dm))


def _layer(x2, batch, seq, l, w_in, conv_w, a_log, dt_bias, gdn_norm_w, lq1, lk1, lq2, lk2,
           diff_norm_w, w_out, ln1_g, ln1_b, w_gate, w_up, w_down, ln2_g, ln2_b, alpha):
    m, dm = x2.shape
    n_gdn_heads = a_log.shape[0]
    gdn_w = n_gdn_heads * GDN_HEAD_DIM
    diff_w = w_out.shape[0] - gdn_w
    n_diff_heads = diff_w // DIFF_V_DIM
    lambda_init = 0.8 - 0.6 * math.exp(-0.3 * l)
    nc = seq // CHUNK

    bn = 768
    main = 4 * gdn_w
    ab = 2 * n_gdn_heads
    wide = -(-(main + ab) // bn) * bn
    w_t = w_in.T
    xb = x2.astype(BF16)

    proj_g = _matmul_wt32(xb, w_t, 0, wide, 1024, bn, F32, "proj_gdn")
    qk_w = 2 * n_diff_heads * 2 * DIFF_HEAD_DIM
    proj_qk, wd_b = _matmul_wt32(xb, w_t, main + ab, qk_w, 1024, 512, BF16, "proj_qk",
                                 side=(w_down, 256))
    proj_vt, wo_b = _matmul_wt32(xb, w_t, main + ab + qk_w, diff_w, 1024, 512, BF16, "proj_vt",
                                 transposed_out=True, side=(w_out, 128))

    gates = proj_g[:, main:main + ab].reshape(batch, seq, ab).transpose(0, 2, 1)
    gates = gates.reshape(batch, ab, nc, CHUNK)
    gates = jnp.pad(gates, ((0, 0), (0, 0), (0, 0), (0, LANES - CHUNK)))
    gdn_out = _gdn(proj_g, conv_w, gates[:, :n_gdn_heads], gates[:, n_gdn_heads:], a_log, dt_bias,
                   gdn_norm_w, batch=batch, seq=seq, n_heads=n_gdn_heads, heads=2)

    diff_out, wg_b, wu_b = _diff_attn(proj_qk, proj_vt, lq1, lk1, lq2, lk2, diff_norm_w, w_gate, w_up,
                                      batch=batch, seq=seq, n_heads=n_diff_heads, tq=512, tk=512,
                                      lambda_init=lambda_init)

    x1 = _out_proj_ln(gdn_out, diff_out, wo_b, x2, ln1_g, ln1_b, bm=512, bk=1024, alpha=alpha)
    return _ffn_ln(x1, wg_b, wu_b, wd_b, ln2_g, ln2_b, bm=512, bf=256, alpha=alpha)


def kernel(x, w_in, conv_w, a_log, dt_bias, gdn_norm_w, lambda_q1, lambda_k1, lambda_q2, lambda_k2,
           diff_norm_w, w_out, ln1_g, ln1_b, w_gate, w_up, w_down, ln2_g, ln2_b):
    batch, seq, dm = x.shape
    depth = w_in.shape[0]
    alpha = (2 * depth) ** 0.25
    x2 = x.reshape(batch * seq, dm)
    for l in range(depth):
        x2 = _layer(x2, batch, seq, l, w_in[l], conv_w[l], a_log[l], dt_bias[l], gdn_norm_w[l],
                    lambda_q1[l], lambda_k1[l], lambda_q2[l], lambda_k2[l], diff_norm_w[l], w_out[l],
                    ln1_g[l], ln1_b[l], w_gate[l], w_up[l], w_down[l], ln2_g[l], ln2_b[l], alpha)
    return x2.reshape(batch, seq, dm)
```
